```python
import jax
import jax.numpy as jnp
from jax import lax
import numpy as np

D_MODEL = 1024
BATCH = 2
SEQ = 8192
DEPTH = 1
DEC_BATCH = 16
DEC_SEQ = 4096
PAST_LEN = 128

ATTN_HEADS = 8
HEAD_DIM = 64
ATTN_WIDTH = ATTN_HEADS * HEAD_DIM
DILATED_PATTERNS = ((128, 1), (512, 4), (2048, 16))
ATTN_BLOCK = 128
SGU_WIDTH = D_MODEL - ATTN_WIDTH
SGU_GROUPS = 4
SGU_GROUP_DIM = SGU_WIDTH // SGU_GROUPS
SGU_CHUNK = 128
MIX_WIDTH = ATTN_WIDTH + SGU_WIDTH
IN_WIDTH = 3 * ATTN_WIDTH + 2 * SGU_WIDTH
D_FF = ((8 * D_MODEL // 3 + 255) // 256) * 256
FFN_RESID = 0.5
N_SUBLAYERS = 3
EPS = 1e-6
NEG = -1e30

kernel_name = 'hybrid_dilated_attn_sgu_encoder'


def _rmsnorm(x, g):
    xf = x.astype(jnp.float32)
    y = xf * lax.rsqrt(jnp.mean(xf * xf, axis=-1, keepdims=True) + EPS)
    return (y * g.astype(jnp.float32)).astype(x.dtype)


def _modulate(h, shift, scale):
    return h * (1 + scale[:, None, :]) + shift[:, None, :]


def _swiglu(h, w_gate, w_up, w_down):
    return (jax.nn.silu(h @ w_gate) * (h @ w_up)) @ w_down


def _alibi_slopes():
    return jnp.exp2(-8.0 * jnp.arange(1, ATTN_HEADS + 1, dtype=jnp.float32) / ATTN_HEADS)


def _dilated_window_attention(q, k, v, window, dil):
    b, s, h, e = q.shape
    n = window // (2 * dil)
    l = s // dil
    nb = -(-l // ATTN_BLOCK)
    lp = nb * ATTN_BLOCK
    kw = ATTN_BLOCK + 2 * n

    def by_residue(t):
        return t.reshape(b, l, dil, h, e).transpose(0, 2, 3, 1, 4)

    qs = jnp.pad(by_residue(q), ((0, 0), (0, 0), (0, 0), (0, lp - l), (0, 0)))
    kpad = ((0, 0), (0, 0), (0, 0), (n, lp - l + n), (0, 0))
    ks = jnp.pad(by_residue(k), kpad)
    vs = jnp.pad(by_residue(v), kpad)
    qb = qs.reshape(b, dil, h, nb, ATTN_BLOCK, e)
    kidx = jnp.arange(nb)[:, None] * ATTN_BLOCK + jnp.arange(kw)[None, :]
    kb = ks[:, :, :, kidx]
    vb = vs[:, :, :, kidx]
    scores = jnp.einsum('bdhnqe,bdhnke->bdhnqk', qb, kb, preferred_element_type=jnp.float32)
    rel = jnp.arange(kw)[None, :] - n - jnp.arange(ATTN_BLOCK)[:, None]
    kpos = jnp.arange(nb)[:, None, None] * ATTN_BLOCK + jnp.arange(kw)[None, None, :] - n
    valid = (jnp.abs(rel) <= n)[None] & (kpos >= 0) & (kpos < l)
    dist = (dil * jnp.abs(rel)).astype(jnp.float32)
    bias = -_alibi_slopes()[:, None, None] * dist[None]
    scores = jnp.where(valid, scores + bias[:, None], NEG)
    m = jnp.max(scores, axis=-1, keepdims=True)
    p = jnp.exp(scores - m)
    denom = jnp.sum(p, axis=-1)
    o = jnp.einsum('bdhnqk,bdhnke->bdhnqe', p, vb.astype(jnp.float32)) / denom[..., None]
    lse = m[..., 0] + jnp.log(denom)
    o = o.reshape(b, dil, h, lp, e)[:, :, :, :l].transpose(0, 3, 1, 2, 4).reshape(b, s, h, e)
    lse = lse.reshape(b, dil, h, lp)[..., :l].transpose(0, 3, 1, 2).reshape(b, s, h)
    return o, lse


def _spatial_gating(u, v, norm_g, w_s, b_s):
    b, s, _ = u.shape
    v = _rmsnorm(v, norm_g)
    vc = v.reshape(b, s // SGU_CHUNK, SGU_CHUNK, SGU_GROUPS, SGU_GROUP_DIM)
    mixed = jnp.einsum('gts,bnsgc->bntgc', w_s, vc) + b_s.T[:, :, None]
    return u * mixed.reshape(b, s, SGU_WIDTH)


def _token_mixing(h, w_in, sgu_norm, sgu_w, sgu_b, w_out):
    b, s, _ = h.shape
    z = h @ w_in
    q, k, v, zu, zv = jnp.split(
        z, [ATTN_WIDTH, 2 * ATTN_WIDTH, 3 * ATTN_WIDTH, 3 * ATTN_WIDTH + SGU_WIDTH], axis=-1)
    q = q.reshape(b, s, ATTN_HEADS, HEAD_DIM) * (HEAD_DIM ** -0.5)
    k = k.reshape(b, s, ATTN_HEADS, HEAD_DIM)
    v = v.reshape(b, s, ATTN_HEADS, HEAD_DIM)
    outs = []
    lses = []
    for window, dil in DILATED_PATTERNS:
        o, lse = _dilated_window_attention(q, k, v, window, dil)
        outs.append(o)
        lses.append(lse)
    weights = jax.nn.softmax(jnp.stack(lses), axis=0)
    attn = jnp.einsum('pbsh,pbshe->bshe', weights, jnp.stack(outs))
    attn = attn.reshape(b, s, ATTN_WIDTH).astype(h.dtype)
    gated = _spatial_gating(jax.nn.gelu(zu), jax.nn.gelu(zv), sgu_norm, sgu_w, sgu_b)
    return jnp.concatenate([attn, gated], axis=-1) @ w_out


def _encode(x, c, ada_w, ada_b, ffn1_norm, ffn1_w_gate, ffn1_w_up, ffn1_w_down,
            mix_norm, w_in, sgu_norm, sgu_w, sgu_b, w_out,
            ffn2_norm, ffn2_w_gate, ffn2_w_up, ffn2_w_down, final_norm):
    b = x.shape[0]
    for i in range(DEPTH):
        mod = (jax.nn.silu(c) @ ada_w[i] + ada_b[i]).reshape(b, N_SUBLAYERS, 3, D_MODEL)
        h = _modulate(_rmsnorm(x, ffn1_norm[i]), mod[:, 0, 0], mod[:, 0, 1])
        x = x + FFN_RESID * mod[:, 0, 2][:, None, :] * _swiglu(h, ffn1_w_gate[i], ffn1_w_up[i], ffn1_w_down[i])
        h = _modulate(_rmsnorm(x, mix_norm[i]), mod[:, 1, 0], mod[:, 1, 1])
        x = x + mod[:, 1, 2][:, None, :] * _token_mixing(h, w_in[i], sgu_norm[i], sgu_w[i], sgu_b[i], w_out[i])
        h = _modulate(_rmsnorm(x, ffn2_norm[i]), mod[:, 2, 0], mod[:, 2, 1])
        x = x + FFN_RESID * mod[:, 2, 2][:, None, :] * _swiglu(h, ffn2_w_gate[i], ffn2_w_up[i], ffn2_w_down[i])
    return _rmsnorm(x, final_norm)


def setup_inputs(seed: int = 0) -> dict:
    key = jax.random.key(seed)
    ks = jax.random.split(key, 24)

    def nrm(k, shape, scale):
        return jax.random.normal(k, shape, jnp.float32) * scale

    def gain(k, shape):
        return 1.0 + nrm(k, shape, 0.05)

    dinv = D_MODEL ** -0.5
    return {
        'x_prompt': nrm(ks[0], (BATCH, SEQ, D_MODEL), 1.0),
        'x_sample': nrm(ks[1], (DEC_BATCH, DEC_SEQ, D_MODEL), 1.0),
        'c_prompt': nrm(ks[2], (BATCH, D_MODEL), 1.0),
        'c_sample': nrm(ks[3], (DEC_BATCH, D_MODEL), 1.0),
        'ada_w': nrm(ks[4], (DEPTH, D_MODEL, N_SUBLAYERS * 3 * D_MODEL), 0.5 * dinv),
        'ada_b': nrm(ks[5], (DEPTH, N_SUBLAYERS * 3 * D_MODEL), 0.01),
        'ffn1_norm': gain(ks[6], (DEPTH, D_MODEL)),
        'ffn1_w_gate': nrm(ks[7], (DEPTH, D_MODEL, D_FF), dinv),
        'ffn1_w_up': nrm(ks[8], (DEPTH, D_MODEL, D_FF), dinv),
        'ffn1_w_down': nrm(ks[9], (DEPTH, D_FF, D_MODEL), D_FF ** -0.5),
        'mix_norm': gain(ks[10], (DEPTH, D_MODEL)),
        'w_in': nrm(ks[11], (DEPTH, D_MODEL, IN_WIDTH), dinv),
        'sgu_norm': gain(ks[12], (DEPTH, SGU_WIDTH)),
        'sgu_w': nrm(ks[13], (DEPTH, SGU_GROUPS, SGU_CHUNK, SGU_CHUNK), SGU_CHUNK ** -0.5),
        'sgu_b': gain(ks[14], (DEPTH, SGU_GROUPS, SGU_CHUNK)),
        'w_out': nrm(ks[15], (DEPTH, MIX_WIDTH, D_MODEL), MIX_WIDTH ** -0.5),
        'ffn2_norm': gain(ks[16], (DEPTH, D_MODEL)),
        'ffn2_w_gate': nrm(ks[17], (DEPTH, D_MODEL, D_FF), dinv),
        'ffn2_w_up': nrm(ks[18], (DEPTH, D_MODEL, D_FF), dinv),
        'ffn2_w_down': nrm(ks[19], (DEPTH, D_FF, D_MODEL), D_FF ** -0.5),
        'final_norm': gain(ks[20], (D_MODEL,)),
    }


def reference(x_prompt, x_sample, c_prompt, c_sample, ada_w, ada_b,
              ffn1_norm, ffn1_w_gate, ffn1_w_up, ffn1_w_down,
              mix_norm, w_in, sgu_norm, sgu_w, sgu_b, w_out,
              ffn2_norm, ffn2_w_gate, ffn2_w_up, ffn2_w_down, final_norm):
    y_prompt = _encode(x_prompt, c_prompt, ada_w, ada_b, ffn1_norm, ffn1_w_gate, ffn1_w_up, ffn1_w_down,
                       mix_norm, w_in, sgu_norm, sgu_w, sgu_b, w_out,
                       ffn2_norm, ffn2_w_gate, ffn2_w_up, ffn2_w_down, final_norm)
    y_sample = _encode(x_sample, c_sample, ada_w, ada_b, ffn1_norm, ffn1_w_gate, ffn1_w_up, ffn1_w_down,
                       mix_norm, w_in, sgu_norm, sgu_w, sgu_b, w_out,
                       ffn2_norm, ffn2_w_gate, ffn2_w_up, ffn2_w_down, final_norm)
    return (y_prompt, y_sample)
```

```python
import functools
import math

import jax
import jax.numpy as jnp
from jax import lax
from jax.experimental import pallas as pl
from jax.experimental.pallas import tpu as pltpu

D_MODEL = 1024
ATTN_HEADS = 8
HEAD_DIM = 64
ATTN_WIDTH = ATTN_HEADS * HEAD_DIM
DILATED_PATTERNS = ((128, 1), (512, 4), (2048, 16))
SGU_WIDTH = D_MODEL - ATTN_WIDTH
SGU_GROUPS = 4
SGU_GROUP_DIM = SGU_WIDTH // SGU_GROUPS
SGU_CHUNK = 128
IN_WIDTH = 3 * ATTN_WIDTH + 2 * SGU_WIDTH
D_FF = 2816
FFN_RESID = 0.5
N_SUBLAYERS = 3
EPS = 1e-6
NEG = -1e30

V7X_LANES = 128
V7X_VMEM_BYTES = 64 * 1024 * 1024
VMEM_LIMIT = V7X_VMEM_BYTES - 8 * 1024 * 1024

TOKEN_TILE = 512
FF_CHUNKS = ((0, 1024), (1024, 1024), (2048, 768))
Q_BLOCK = 128
HALO = 64
K_BLOCK = Q_BLOCK + 2 * HALO
HEAD_PAIR = 2 * HEAD_DIM

assert all(w // (2 * d) == HALO for w, d in DILATED_PATTERNS)
assert HEAD_PAIR == V7X_LANES


def _const_spec(shape):
    zeros = (0,) * len(shape)
    return pl.BlockSpec(shape, lambda *_: zeros, pipeline_mode=pl.Buffered(1))


def _rms(x):
    return x * lax.rsqrt(jnp.mean(x * x, axis=-1, keepdims=True) + EPS)


def _silu(x):
    return x * (1.0 / (1.0 + jnp.exp(-x)))


def _gelu_tanh(x):
    c = math.sqrt(2.0 / math.pi)
    return 0.5 * x * (1.0 + jnp.tanh(c * (x + 0.044715 * (x * x * x))))


def _dot(a, b):
    return jnp.dot(a, b, preferred_element_type=jnp.float32)


def _swiglu(h, wg_ref, wu_ref, wd_ref):
    acc = None
    for start, size in FF_CHUNKS:
        g = _dot(h, wg_ref[:, start:start + size])
        u = _dot(h, wu_ref[:, start:start + size])
        a = (_silu(g) * u).astype(jnp.bfloat16)
        part = _dot(a, wd_ref[start:start + size, :])
        acc = part if acc is None else acc + part
    return acc


def _mod_kernel(c_ref, w_ref, b_ref, o_ref):
    c = c_ref[...]
    o_ref[...] = jnp.dot(_silu(c), w_ref[...], preferred_element_type=jnp.float32,
                         precision=lax.Precision.HIGHEST) + b_ref[...]


def _modulation(c_all, ada_w, ada_b):
    nb = c_all.shape[0]
    width = ada_w.shape[1]
    col = 1536
    return pl.pallas_call(
        _mod_kernel,
        grid=(width // col,),
        in_specs=[pl.BlockSpec((nb, D_MODEL), lambda j: (0, 0)),
                  pl.BlockSpec((D_MODEL, col), lambda j: (0, j)),
                  pl.BlockSpec((1, col), lambda j: (0, j))],
        out_specs=pl.BlockSpec((nb, col), lambda j: (0, j)),
        out_shape=jax.ShapeDtypeStruct((nb, width), jnp.float32),
        name="adaln_mod",
    )(c_all, ada_w, ada_b.reshape(1, width))


def _front_kernel(x_ref, mod_ref, n1_ref, wg_ref, wu_ref, wd_ref, nm_ref, win_ref,
                  sn_ref, sw_ref, sb_ref,
                  x1_ref, q_ref, k_ref, v_ref, gated_ref):
    x = x_ref[...]
    shift0, scale0, gate0 = mod_ref[0:1, :], mod_ref[1:2, :], mod_ref[2:3, :]
    shift1, scale1 = mod_ref[3:4, :], mod_ref[4:5, :]

    h = (_rms(x) * n1_ref[...]) * (1.0 + scale0) + shift0
    ffn = _swiglu(h.astype(jnp.bfloat16), wg_ref, wu_ref, wd_ref)
    x1 = x + (FFN_RESID * gate0) * ffn
    x1_ref[...] = x1

    h = (_rms(x1) * nm_ref[...]) * (1.0 + scale1) + shift1
    z = _dot(h.astype(jnp.bfloat16), win_ref[...])
    q_ref[...] = (z[:, 0:ATTN_WIDTH] * (HEAD_DIM ** -0.5)).astype(jnp.bfloat16)
    k_ref[...] = z[:, ATTN_WIDTH:2 * ATTN_WIDTH].astype(jnp.bfloat16)
    v_ref[...] = z[:, 2 * ATTN_WIDTH:3 * ATTN_WIDTH].astype(jnp.bfloat16)

    u = _gelu_tanh(z[:, 3 * ATTN_WIDTH:3 * ATTN_WIDTH + SGU_WIDTH])
    sv = _gelu_tanh(z[:, 3 * ATTN_WIDTH + SGU_WIDTH:])
    sv = (_rms(sv) * sn_ref[...]).astype(jnp.bfloat16)
    for n in range(TOKEN_TILE // SGU_CHUNK):
        rows = slice(n * SGU_CHUNK, (n + 1) * SGU_CHUNK)
        mixed = jnp.concatenate(
            [_dot(sw_ref[g], sv[rows, g * SGU_GROUP_DIM:(g + 1) * SGU_GROUP_DIM])
             for g in range(SGU_GROUPS)], axis=1) + sb_ref[...]
        gated_ref[rows, :] = (u[rows, :] * mixed).astype(jnp.bfloat16)


def _front(x2d, mod3, batch_offset, tiles_per_batch, n1, wg, wu, wd, nm, win, sn, sw, sb_full):
    n_tok = x2d.shape[0]
    tm = TOKEN_TILE
    row_spec = lambda width: pl.BlockSpec((tm, width), lambda i: (i, 0))
    return pl.pallas_call(
        _front_kernel,
        grid=(n_tok // tm,),
        in_specs=[row_spec(D_MODEL),
                  pl.BlockSpec((None, 3 * N_SUBLAYERS, D_MODEL),
                               lambda i: (batch_offset + i // tiles_per_batch, 0, 0)),
                  _const_spec((1, D_MODEL)),
                  _const_spec((D_MODEL, D_FF)), _const_spec((D_MODEL, D_FF)),
                  _const_spec((D_FF, D_MODEL)),
                  _const_spec((1, D_MODEL)),
                  _const_spec((D_MODEL, IN_WIDTH)),
                  _const_spec((1, SGU_WIDTH)),
                  _const_spec((SGU_GROUPS, SGU_CHUNK, SGU_CHUNK)),
                  _const_spec((SGU_CHUNK, SGU_WIDTH))],
        out_specs=[row_spec(D_MODEL), row_spec(ATTN_WIDTH), row_spec(ATTN_WIDTH),
                   row_spec(ATTN_WIDTH), row_spec(SGU_WIDTH)],
        out_shape=[jax.ShapeDtypeStruct((n_tok, D_MODEL), jnp.float32),
                   jax.ShapeDtypeStruct((n_tok, ATTN_WIDTH), jnp.bfloat16),
                   jax.ShapeDtypeStruct((n_tok, ATTN_WIDTH), jnp.bfloat16),
                   jax.ShapeDtypeStruct((n_tok, ATTN_WIDTH), jnp.bfloat16),
                   jax.ShapeDtypeStruct((n_tok, SGU_WIDTH), jnp.bfloat16)],
        compiler_params=pltpu.CompilerParams(
            dimension_semantics=("arbitrary",), vmem_limit_bytes=VMEM_LIMIT),
        name="front_ffn1_inproj_sgu",
    )(x2d, mod3, n1, wg, wu, wd, nm, win, sn, sw, sb_full)


def _attn_kernel(q_ref, kp_ref, km_ref, kn_ref, vp_ref, vm_ref, vn_ref,
                 o_ref, lse_ref, kwin, vwin, bias_ref, *, dil, tq, seq_len):
    i = pl.program_id(2)
    first_step = (pl.program_id(0) == 0) & (pl.program_id(1) == 0) & (i == 0)

    @pl.when(first_step)
    def _():
        row = lax.broadcasted_iota(jnp.int32, (Q_BLOCK, K_BLOCK), 0)
        col = lax.broadcasted_iota(jnp.int32, (Q_BLOCK, K_BLOCK), 1)
        rel = jnp.abs(col - HALO - row)
        dist = (dil * rel).astype(jnp.float32)
        for h in range(ATTN_HEADS):
            slope = 2.0 ** (-8.0 * (h + 1) / ATTN_HEADS)
            bias_ref[h] = jnp.where(rel <= HALO, -slope * dist, NEG)

    for win, prev, main, nxt in ((kwin, kp_ref, km_ref, kn_ref), (vwin, vp_ref, vm_ref, vn_ref)):
        win[0:HALO, :] = prev[...]
        win[HALO:HALO + tq, :] = main[...]
        win[HALO + tq:HALO + tq + HALO, :] = nxt[...]

    lane = lax.broadcasted_iota(jnp.int32, (1, HEAD_PAIR), 1)
    first_head = lane < HEAD_DIM
    col_row = lax.broadcasted_iota(jnp.int32, (1, K_BLOCK), 1)
    n_blocks = tq // Q_BLOCK

    for j in range(n_blocks):
        edge = None
        if j == 0 or j == n_blocks - 1:
            kpos = i * tq + (j * Q_BLOCK - HALO) + col_row
            edge = jnp.where((kpos < 0) | (kpos >= seq_len), NEG, 0.0)
        qrows = slice(j * Q_BLOCK, (j + 1) * Q_BLOCK)
        krows = slice(j * Q_BLOCK, j * Q_BLOCK + K_BLOCK)
        for pair in range(ATTN_HEADS // 2):
            cols = slice(pair * HEAD_PAIR, (pair + 1) * HEAD_PAIR)
            qp = q_ref[qrows, cols]
            kp = kwin[krows, cols]
            vp = vwin[krows, cols]
            acc = None
            m_pair = l_pair = None
            for hh in range(2):
                sel = first_head if hh == 0 else jnp.logical_not(first_head)
                qh = jnp.where(sel, qp, jnp.zeros_like(qp))
                s = lax.dot_general(qh, kp, (((1,), (1,)), ((), ())),
                                    preferred_element_type=jnp.float32)
                s = s + bias_ref[2 * pair + hh]
                if edge is not None:
                    s = s + edge
                m = jnp.max(s, axis=-1, keepdims=True)
                p = jnp.exp(s - m)
                l = jnp.sum(p, axis=-1, keepdims=True)
                vh = jnp.where(sel, vp, jnp.zeros_like(vp))
                part = _dot(p.astype(jnp.bfloat16), vh)
                if hh == 0:
                    acc = part
                    m_pair = jnp.broadcast_to(m, (Q_BLOCK, HEAD_PAIR))
                    l_pair = jnp.broadcast_to(l, (Q_BLOCK, HEAD_PAIR))
                else:
                    acc = acc + part
                    m_pair = jnp.where(first_head, m_pair, m)
                    l_pair = jnp.where(first_head, l_pair, l)
            o_ref[qrows, cols] = (acc / l_pair).astype(o_ref.dtype)
            lse_ref[qrows, cols] = m_pair + jnp.log(l_pair)


def _attention(q, k, v, dil):
    b, s, _ = q.shape
    seq_len = s // dil
    tq = min(seq_len, 512)
    halo_blocks = tq // HALO
    last_halo = seq_len // HALO - 1
    view = lambda t: t.reshape(b, seq_len, dil * ATTN_WIDTH)
    main = pl.BlockSpec((None, tq, ATTN_WIDTH), lambda bi, r, i: (bi, i, r))
    prev = pl.BlockSpec((None, HALO, ATTN_WIDTH),
                        lambda bi, r, i: (bi, jnp.maximum(i * halo_blocks - 1, 0), r))
    nxt = pl.BlockSpec((None, HALO, ATTN_WIDTH),
                       lambda bi, r, i: (bi, jnp.minimum((i + 1) * halo_blocks, last_halo), r))
    qv, kv, vv = view(q), view(k), view(v)
    o, lse = pl.pallas_call(
        functools.partial(_attn_kernel, dil=dil, tq=tq, seq_len=seq_len),
        grid=(b, dil, seq_len // tq),
        in_specs=[main, prev, main, nxt, prev, main, nxt],
        out_specs=[main, main],
        out_shape=[jax.ShapeDtypeStruct(qv.shape, jnp.bfloat16),
                   jax.ShapeDtypeStruct(qv.shape, jnp.float32)],
        scratch_shapes=[pltpu.VMEM((tq + 2 * HALO, ATTN_WIDTH), jnp.bfloat16),
                        pltpu.VMEM((tq + 2 * HALO, ATTN_WIDTH), jnp.bfloat16),
                        pltpu.VMEM((ATTN_HEADS, Q_BLOCK, K_BLOCK), jnp.float32)],
        compiler_params=pltpu.CompilerParams(
            dimension_semantics=("arbitrary", "arbitrary", "arbitrary")),
        name=f"band_attn_d{dil}",
    )(qv, kv, kv, kv, vv, vv, vv)
    return o.reshape(b * s, ATTN_WIDTH), lse.reshape(b * s, ATTN_WIDTH)


def _back_kernel(x1_ref, mod_ref, gated_ref, o0_ref, o1_ref, o2_ref, l0_ref, l1_ref, l2_ref,
                 wout_ref, n2_ref, wg_ref, wu_ref, wd_ref, nf_ref, y_ref):
    gate1 = mod_ref[5:6, :]
    shift2, scale2, gate2 = mod_ref[6:7, :], mod_ref[7:8, :], mod_ref[8:9, :]

    l0, l1, l2 = l0_ref[...], l1_ref[...], l2_ref[...]
    top = jnp.maximum(jnp.maximum(l0, l1), l2)
    w0, w1, w2 = jnp.exp(l0 - top), jnp.exp(l1 - top), jnp.exp(l2 - top)
    attn = (w0 * o0_ref[...].astype(jnp.float32) + w1 * o1_ref[...].astype(jnp.float32)
            + w2 * o2_ref[...].astype(jnp.float32)) / (w0 + w1 + w2)
    mix = (_dot(attn.astype(jnp.bfloat16), wout_ref[0:ATTN_WIDTH, :])
           + _dot(gated_ref[...], wout_ref[ATTN_WIDTH:, :]))
    x2 = x1_ref[...] + gate1 * mix

    h = (_rms(x2) * n2_ref[...]) * (1.0 + scale2) + shift2
    ffn = _swiglu(h.astype(jnp.bfloat16), wg_ref, wu_ref, wd_ref)
    x3 = x2 + (FFN_RESID * gate2) * ffn
    y_ref[...] = _rms(x3) * nf_ref[...]


def _back(x1, mod3, batch_offset, tiles_per_batch, gated, outs, lses, wout, n2, wg, wu, wd, nf):
    n_tok = x1.shape[0]
    tm = TOKEN_TILE
    row_spec = lambda width: pl.BlockSpec((tm, width), lambda i: (i, 0))
    return pl.pallas_call(
        _back_kernel,
        grid=(n_tok // tm,),
        in_specs=[row_spec(D_MODEL),
                  pl.BlockSpec((None, 3 * N_SUBLAYERS, D_MODEL),
                               lambda i: (batch_offset + i // tiles_per_batch, 0, 0)),
                  row_spec(SGU_WIDTH)]
                 + [row_spec(ATTN_WIDTH)] * 6
                 + [_const_spec((D_MODEL, D_MODEL)),
                    _const_spec((1, D_MODEL)),
                    _const_spec((D_MODEL, D_FF)), _const_spec((D_MODEL, D_FF)),
                    _const_spec((D_FF, D_MODEL)),
                    _const_spec((1, D_MODEL))],
        out_specs=row_spec(D_MODEL),
        out_shape=jax.ShapeDtypeStruct((n_tok, D_MODEL), jnp.float32),
        compiler_params=pltpu.CompilerParams(
            dimension_semantics=("arbitrary",), vmem_limit_bytes=VMEM_LIMIT),
        name="back_merge_outproj_ffn2",
    )(x1, mod3, gated, *outs, *lses, wout, n2, wg, wu, wd, nf)


def kernel(x_prompt, x_sample, c_prompt, c_sample, ada_w, ada_b, ffn1_norm, ffn1_w_gate, ffn1_w_up, ffn1_w_down, mix_norm, w_in, sgu_norm, sgu_w, sgu_b, w_out, ffn2_norm, ffn2_w_gate, ffn2_w_up, ffn2_w_down, final_norm):
    bf = lambda w: w.astype(jnp.bfloat16)
    row = lambda g: g.reshape(1, -1)
    assert ada_w.shape[0] == 1, "single-layer stack"

    c_all = jnp.concatenate([c_prompt, c_sample], axis=0)
    mod3 = _modulation(c_all, ada_w[0], ada_b[0]).reshape(-1, 3 * N_SUBLAYERS, D_MODEL)

    wg1, wu1, wd1 = bf(ffn1_w_gate[0]), bf(ffn1_w_up[0]), bf(ffn1_w_down[0])
    wg2, wu2, wd2 = bf(ffn2_w_gate[0]), bf(ffn2_w_up[0]), bf(ffn2_w_down[0])
    win, wout, sw = bf(w_in[0]), bf(w_out[0]), bf(sgu_w[0])
    sb_full = jnp.repeat(sgu_b[0].T, SGU_GROUP_DIM, axis=1)

    def encode(x, batch_offset):
        b, s, _ = x.shape
        tiles_per_batch = s // TOKEN_TILE
        x2d = x.reshape(b * s, D_MODEL)
        x1, q, k, v, gated = _front(
            x2d, mod3, batch_offset, tiles_per_batch, row(ffn1_norm[0]), wg1, wu1, wd1,
            row(mix_norm[0]), win, row(sgu_norm[0]), sw, sb_full)
        shape3 = (b, s, ATTN_WIDTH)
        outs, lses = [], []
        for _, dil in DILATED_PATTERNS:
            o, lse = _attention(q.reshape(shape3), k.reshape(shape3), v.reshape(shape3), dil)
            outs.append(o)
            lses.append(lse)
        y = _back(x1, mod3, batch_offset, tiles_per_batch, gated, outs, lses, wout,
                  row(ffn2_norm[0]), wg2, wu2, wd2, row(final_norm))
        return y.reshape(b, s, D_MODEL)

    y_prompt = encode(x_prompt, 0)
    y_sample = encode(x_sample, x_prompt.shape[0])
    return (y_prompt, y_sample)
```

```python
import functools
import math

import jax
import jax.numpy as jnp
from jax import lax
from jax.experimental import pallas as pl
from jax.experimental.pallas import tpu as pltpu

D_MODEL = 1024
ATTN_HEADS = 8
HEAD_DIM = 64
ATTN_WIDTH = ATTN_HEADS * HEAD_DIM
DILATED_PATTERNS = ((128, 1), (512, 4), (2048, 16))
SGU_WIDTH = D_MODEL - ATTN_WIDTH
SGU_GROUPS = 4
SGU_GROUP_DIM = SGU_WIDTH // SGU_GROUPS
SGU_CHUNK = 128
IN_WIDTH = 3 * ATTN_WIDTH + 2 * SGU_WIDTH
D_FF = 2816
FFN_RESID = 0.5
N_SUBLAYERS = 3
EPS = 1e-6
NEG = -1e30
LOG2E = math.log2(math.e)

V7X_LANES = 128
V7X_VMEM_BYTES = 64 * 1024 * 1024
VMEM_LIMIT = V7X_VMEM_BYTES - 8 * 1024 * 1024

TOKEN_TILE = 512
FF_CHUNKS = ((0, 1024), (1024, 1024), (2048, 768))
Q_BLOCK = 128
HALO = 64
K_BLOCK = Q_BLOCK + 2 * HALO
HEAD_PAIR = 2 * HEAD_DIM
N_PAIRS = ATTN_HEADS // 2
QKV_SLABS = 3 * ATTN_WIDTH // V7X_LANES
RES_STEP = 4

assert all(w // (2 * d) == HALO for w, d in DILATED_PATTERNS)
assert [d for _, d in DILATED_PATTERNS] == [1, RES_STEP, RES_STEP * RES_STEP]
assert HEAD_PAIR == V7X_LANES


def _const_spec(shape):
    zeros = (0,) * len(shape)
    return pl.BlockSpec(shape, lambda *_: zeros, pipeline_mode=pl.Buffered(1))


def _rms(x):
    return x * lax.rsqrt(jnp.mean(x * x, axis=-1, keepdims=True) + EPS)


def _silu(x):
    return x * (1.0 / (1.0 + jnp.exp(-x)))


def _gelu_tanh(x):
    c = math.sqrt(2.0 / math.pi)
    return 0.5 * x * (1.0 + jnp.tanh(c * (x + 0.044715 * (x * x * x))))


def _dot(a, b):
    return jnp.dot(a, b, preferred_element_type=jnp.float32)


def _swiglu(h, wg_ref, wu_ref, wd_ref):
    acc = None
    for start, size in FF_CHUNKS:
        g = _dot(h, wg_ref[:, start:start + size])
        u = _dot(h, wu_ref[:, start:start + size])
        a = (_silu(g) * u).astype(jnp.bfloat16)
        part = _dot(a, wd_ref[start:start + size, :])
        acc = part if acc is None else acc + part
    return acc


def _mod_kernel(c_ref, w_ref, b_ref, o_ref):
    c = c_ref[...]
    o_ref[...] = jnp.dot(_silu(c), w_ref[...], preferred_element_type=jnp.float32,
                         precision=lax.Precision.HIGHEST) + b_ref[...]


def _modulation(c_all, ada_w, ada_b):
    nb = c_all.shape[0]
    width = ada_w.shape[1]
    col = 1536
    return pl.pallas_call(
        _mod_kernel,
        grid=(width // col,),
        in_specs=[pl.BlockSpec((nb, D_MODEL), lambda j: (0, 0)),
                  pl.BlockSpec((D_MODEL, col), lambda j: (0, j)),
                  pl.BlockSpec((1, col), lambda j: (0, j))],
        out_specs=pl.BlockSpec((nb, col), lambda j: (0, j)),
        out_shape=jax.ShapeDtypeStruct((nb, width), jnp.float32),
        name="adaln_mod",
    )(c_all, ada_w, ada_b.reshape(1, width))


def _front_kernel(x_ref, mod_ref, n1_ref, wg_ref, wu_ref, wd_ref, nm_ref, win_ref,
                  sn_ref, sw_ref, sb_ref,
                  x1_ref, gated_ref, q1_ref, k1_ref, v1_ref, q4_ref, k4_ref, v4_ref,
                  q16_ref, k16_ref, v16_ref, zs_ref, zs4_ref):
    x = x_ref[...]
    shift0, scale0, gate0 = mod_ref[0:1, :], mod_ref[1:2, :], mod_ref[2:3, :]
    shift1, scale1 = mod_ref[3:4, :], mod_ref[4:5, :]

    h = (_rms(x) * n1_ref[...]) * (1.0 + scale0) + shift0
    ffn = _swiglu(h.astype(jnp.bfloat16), wg_ref, wu_ref, wd_ref)
    x1 = x + (FFN_RESID * gate0) * ffn
    x1_ref[...] = x1

    h = (_rms(x1) * nm_ref[...]) * (1.0 + scale1) + shift1
    z = _dot(h.astype(jnp.bfloat16), win_ref[...])

    u = _gelu_tanh(z[:, 3 * ATTN_WIDTH:3 * ATTN_WIDTH + SGU_WIDTH])
    sv = _gelu_tanh(z[:, 3 * ATTN_WIDTH + SGU_WIDTH:])
    sv = (_rms(sv) * sn_ref[...]).astype(jnp.bfloat16)
    for n in range(TOKEN_TILE // SGU_CHUNK):
        rows = slice(n * SGU_CHUNK, (n + 1) * SGU_CHUNK)
        mixed = jnp.concatenate(
            [_dot(sw_ref[g], sv[rows, g * SGU_GROUP_DIM:(g + 1) * SGU_GROUP_DIM])
             for g in range(SGU_GROUPS)], axis=1) + sb_ref[...]
        gated_ref[rows, :] = (u[rows, :] * mixed).astype(jnp.bfloat16)

    slabs_per = ATTN_WIDTH // V7X_LANES
    for s in range(QKV_SLABS):
        zslab = z[:, s * V7X_LANES:(s + 1) * V7X_LANES]
        if s < slabs_per:
            zslab = zslab * (HEAD_DIM ** -0.5 * LOG2E)
        zs_ref[s] = zslab
    outs1 = (q1_ref, k1_ref, v1_ref)
    outs4 = (q4_ref, k4_ref, v4_ref)
    outs16 = (q16_ref, k16_ref, v16_ref)
    rows4 = TOKEN_TILE // RES_STEP
    rows16 = rows4 // RES_STEP
    for s in range(QKV_SLABS):
        t, ss = divmod(s, slabs_per)
        lanes = slice(ss * V7X_LANES, (ss + 1) * V7X_LANES)
        outs1[t][:, lanes] = zs_ref[s].astype(jnp.bfloat16)
        for c in range(RES_STEP):
            piece = zs_ref[s, pl.ds(c, rows4, stride=RES_STEP), :]
            zs4_ref[c * QKV_SLABS + s] = piece
            base = c * ATTN_WIDTH + ss * V7X_LANES
            outs4[t][:, base:base + V7X_LANES] = piece.astype(jnp.bfloat16)
    for s in range(QKV_SLABS):
        t, ss = divmod(s, slabs_per)
        for c in range(RES_STEP):
            for a in range(RES_STEP):
                piece = zs4_ref[c * QKV_SLABS + s, pl.ds(a, rows16, stride=RES_STEP), :]
                base = (c * RES_STEP + a) * ATTN_WIDTH + ss * V7X_LANES
                outs16[t][:, base:base + V7X_LANES] = piece.astype(jnp.bfloat16)


def _front(x2d, mod3, batch_offset, tiles_per_batch, n1, wg, wu, wd, nm, win, sn, sw, sb_full):
    n_tok = x2d.shape[0]
    tm = TOKEN_TILE
    row_spec = lambda rows, width: pl.BlockSpec((rows, width), lambda i: (i, 0))
    bf16 = jnp.bfloat16
    views = [(tm, n_tok, ATTN_WIDTH),
             (tm // RES_STEP, n_tok // RES_STEP, RES_STEP * ATTN_WIDTH),
             (tm // RES_STEP ** 2, n_tok // RES_STEP ** 2, RES_STEP ** 2 * ATTN_WIDTH)]
    qkv_specs = [row_spec(r, w) for r, _, w in views for _ in range(3)]
    qkv_shapes = [jax.ShapeDtypeStruct((n, w), bf16) for _, n, w in views for _ in range(3)]
    return pl.pallas_call(
        _front_kernel,
        grid=(n_tok // tm,),
        in_specs=[row_spec(tm, D_MODEL),
                  pl.BlockSpec((None, 3 * N_SUBLAYERS, D_MODEL),
                               lambda i: (batch_offset + i // tiles_per_batch, 0, 0)),
                  _const_spec((1, D_MODEL)),
                  _const_spec((D_MODEL, D_FF)), _const_spec((D_MODEL, D_FF)),
                  _const_spec((D_FF, D_MODEL)),
                  _const_spec((1, D_MODEL)),
                  _const_spec((D_MODEL, IN_WIDTH)),
                  _const_spec((1, SGU_WIDTH)),
                  _const_spec((SGU_GROUPS, SGU_CHUNK, SGU_CHUNK)),
                  _const_spec((SGU_CHUNK, SGU_WIDTH))],
        out_specs=[row_spec(tm, D_MODEL), row_spec(tm, SGU_WIDTH)] + qkv_specs,
        out_shape=[jax.ShapeDtypeStruct((n_tok, D_MODEL), jnp.float32),
                   jax.ShapeDtypeStruct((n_tok, SGU_WIDTH), bf16)] + qkv_shapes,
        scratch_shapes=[pltpu.VMEM((QKV_SLABS, tm, V7X_LANES), jnp.float32),
                        pltpu.VMEM((RES_STEP * QKV_SLABS, tm // RES_STEP, V7X_LANES), jnp.float32)],
        compiler_params=pltpu.CompilerParams(
            dimension_semantics=("arbitrary",), vmem_limit_bytes=VMEM_LIMIT),
        name="front_ffn1_inproj_sgu",
    )(x2d, mod3, n1, wg, wu, wd, nm, win, sn, sw, sb_full)


def _attn_kernel(*refs, dil, tq, seq_len, n_res, has_prev, final):
    refs = list(refs)
    q_ref, kp_ref, km_ref, kn_ref, vp_ref, vm_ref, vn_ref = refs[:7]
    pos = 7
    if has_prev:
        acc_p, m_p, l_p = refs[pos:pos + 3]
        pos += 3
    if final:
        o_ref = refs[pos]
        pos += 1
    else:
        acc_o, m_o, l_o = refs[pos:pos + 3]
        pos += 3
    kwin, vwin, bias_ref = refs[pos:pos + 3]
    pos += 3
    interleave = (not final) and n_res > 1
    if interleave:
        acc_s, m_s, l_s = refs[pos:pos + 3]

    i = pl.program_id(2)
    first_step = (pl.program_id(0) == 0) & (pl.program_id(1) == 0) & (i == 0)

    @pl.when(first_step)
    def _():
        row = lax.broadcasted_iota(jnp.int32, (Q_BLOCK, K_BLOCK), 0)
        col = lax.broadcasted_iota(jnp.int32, (Q_BLOCK, K_BLOCK), 1)
        rel = jnp.abs(col - HALO - row)
        dist = (dil * rel).astype(jnp.float32)
        for h in range(ATTN_HEADS):
            slope = 2.0 ** (-8.0 * (h + 1) / ATTN_HEADS)
            bias_ref[h] = jnp.where(rel <= HALO, (-slope * LOG2E) * dist, NEG)

    for win, prev, main, nxt in ((kwin, kp_ref, km_ref, kn_ref), (vwin, vp_ref, vm_ref, vn_ref)):
        win[0:HALO, :] = prev[...]
        win[HALO:HALO + tq, :] = main[...]
        win[HALO + tq:HALO + tq + HALO, :] = nxt[...]

    lane = lax.broadcasted_iota(jnp.int32, (1, HEAD_PAIR), 1)
    first_head = lane < HEAD_DIM
    col_row = lax.broadcasted_iota(jnp.int32, (1, K_BLOCK), 1)
    n_blocks = tq // Q_BLOCK

    for res in range(n_res):
        for j in range(n_blocks):
            edge = None
            if j == 0 or j == n_blocks - 1:
                kpos = i * tq + (j * Q_BLOCK - HALO) + col_row
                edge = jnp.where((kpos < 0) | (kpos >= seq_len), NEG, 0.0)
            qrows = slice(j * Q_BLOCK, (j + 1) * Q_BLOCK)
            krows = slice(j * Q_BLOCK, j * Q_BLOCK + K_BLOCK)
            for pair in range(N_PAIRS):
                cols = slice(res * ATTN_WIDTH + pair * HEAD_PAIR,
                             res * ATTN_WIDTH + (pair + 1) * HEAD_PAIR)
                qp = q_ref[qrows, cols]
                kp = kwin[krows, cols]
                vp = vwin[krows, cols]
                acc = None
                m_pair = l_pair = None
                for hh in range(2):
                    sel = first_head if hh == 0 else jnp.logical_not(first_head)
                    qh = jnp.where(sel, qp, jnp.zeros_like(qp))
                    s = lax.dot_general(qh, kp, (((1,), (1,)), ((), ())),
                                        preferred_element_type=jnp.float32)
                    s = s + bias_ref[2 * pair + hh]
                    if edge is not None:
                        s = s + edge
                    m = jnp.max(s, axis=-1, keepdims=True)
                    p = jnp.exp2(s - m)
                    l = jnp.sum(p, axis=-1, keepdims=True)
                    vh = jnp.where(sel, vp, jnp.zeros_like(vp))
                    part = _dot(p.astype(jnp.bfloat16), vh)
                    if hh == 0:
                        acc = part
                        m_pair = jnp.broadcast_to(m, (Q_BLOCK, HEAD_PAIR))
                        l_pair = jnp.broadcast_to(l, (Q_BLOCK, HEAD_PAIR))
                    else:
                        acc = acc + part
                        m_pair = jnp.where(first_head, m_pair, m)
                        l_pair = jnp.where(first_head, l_pair, l)
                if has_prev:
                    a_prev = acc_p[qrows, cols].astype(jnp.float32)
                    m_prev = m_p[qrows, cols]
                    m_new = jnp.maximum(m_prev, m_pair)
                    alpha = jnp.exp2(m_prev - m_new)
                    beta = jnp.exp2(m_pair - m_new)
                    l_pair = alpha * l_p[qrows, cols] + beta * l_pair
                    acc = alpha * a_prev + beta * acc
                    m_pair = m_new
                if final:
                    o_ref[qrows, cols] = (acc / l_pair).astype(o_ref.dtype)
                elif interleave:
                    dst = pl.ds(n_res * j * Q_BLOCK + res, Q_BLOCK, stride=n_res)
                    acc_s[pair, dst, :] = acc
                    m_s[pair, dst, :] = m_pair
                    l_s[pair, dst, :] = l_pair
                else:
                    acc_o[qrows, cols] = acc.astype(acc_o.dtype)
                    m_o[qrows, cols] = m_pair
                    l_o[qrows, cols] = l_pair

    if interleave:
        for pair in range(N_PAIRS):
            cols = slice(pair * HEAD_PAIR, (pair + 1) * HEAD_PAIR)
            acc_o[:, cols] = acc_s[pair].astype(acc_o.dtype)
            m_o[:, cols] = m_s[pair]
            l_o[:, cols] = l_s[pair]


def _attention(q, k, v, prev, *, dil, n_res, tq, final):
    b, seq_len, width = q.shape
    n_groups = width // (n_res * ATTN_WIDTH)
    blk_w = n_res * ATTN_WIDTH
    halo_blocks = tq // HALO
    last_halo = seq_len // HALO - 1
    main = pl.BlockSpec((None, tq, blk_w), lambda bi, g, i: (bi, i, g))
    prv = pl.BlockSpec((None, HALO, blk_w),
                       lambda bi, g, i: (bi, jnp.maximum(i * halo_blocks - 1, 0), g))
    nxt = pl.BlockSpec((None, HALO, blk_w),
                       lambda bi, g, i: (bi, jnp.minimum((i + 1) * halo_blocks, last_halo), g))
    out_rows = n_res * tq
    out_spec = pl.BlockSpec((None, out_rows, ATTN_WIDTH), lambda bi, g, i: (bi, i, g))
    out_dims = (b, seq_len * n_res, n_groups * ATTN_WIDTH)

    in_specs = [main, prv, main, nxt, prv, main, nxt]
    args = [q, k, k, k, v, v, v]
    if prev is not None:
        in_specs += [main, main, main]
        args += list(prev)
    scratch = [pltpu.VMEM((tq + 2 * HALO, blk_w), jnp.bfloat16),
               pltpu.VMEM((tq + 2 * HALO, blk_w), jnp.bfloat16),
               pltpu.VMEM((ATTN_HEADS, Q_BLOCK, K_BLOCK), jnp.float32)]
    if final:
        out_specs = [out_spec]
        out_shape = [jax.ShapeDtypeStruct(out_dims, jnp.bfloat16)]
    else:
        out_specs = [out_spec] * 3
        out_shape = [jax.ShapeDtypeStruct(out_dims, jnp.bfloat16),
                     jax.ShapeDtypeStruct(out_dims, jnp.float32),
                     jax.ShapeDtypeStruct(out_dims, jnp.float32)]
        if n_res > 1:
            scratch += [pltpu.VMEM((N_PAIRS, out_rows, HEAD_PAIR), jnp.float32)] * 3
    outs = pl.pallas_call(
        functools.partial(_attn_kernel, dil=dil, tq=tq, seq_len=seq_len, n_res=n_res,
                          has_prev=prev is not None, final=final),
        grid=(b, n_groups, seq_len // tq),
        in_specs=in_specs,
        out_specs=out_specs,
        out_shape=out_shape,
        scratch_shapes=scratch,
        compiler_params=pltpu.CompilerParams(
            dimension_semantics=("arbitrary", "arbitrary", "arbitrary"),
            vmem_limit_bytes=VMEM_LIMIT),
        name=f"band_attn_d{dil}",
    )(*args)
    return outs[0] if final else tuple(outs)


def _back_kernel(x1_ref, mod_ref, gated_ref, attn_ref,
                 wout_ref, n2_ref, wg_ref, wu_ref, wd_ref, nf_ref, y_ref):
    gate1 = mod_ref[5:6, :]
    shift2, scale2, gate2 = mod_ref[6:7, :], mod_ref[7:8, :], mod_ref[8:9, :]

    mix = (_dot(attn_ref[...], wout_ref[0:ATTN_WIDTH, :])
           + _dot(gated_ref[...], wout_ref[ATTN_WIDTH:, :]))
    x2 = x1_ref[...] + gate1 * mix

    h = (_rms(x2) * n2_ref[...]) * (1.0 + scale2) + shift2
    ffn = _swiglu(h.astype(jnp.bfloat16), wg_ref, wu_ref, wd_ref)
    x3 = x2 + (FFN_RESID * gate2) * ffn
    y_ref[...] = _rms(x3) * nf_ref[...]


def _back(x1, mod3, batch_offset, tiles_per_batch, gated, attn, wout, n2, wg, wu, wd, nf):
    n_tok = x1.shape[0]
    tm = TOKEN_TILE
    row_spec = lambda width: pl.BlockSpec((tm, width), lambda i: (i, 0))
    return pl.pallas_call(
        _back_kernel,
        grid=(n_tok // tm,),
        in_specs=[row_spec(D_MODEL),
                  pl.BlockSpec((None, 3 * N_SUBLAYERS, D_MODEL),
                               lambda i: (batch_offset + i // tiles_per_batch, 0, 0)),
                  row_spec(SGU_WIDTH), row_spec(ATTN_WIDTH),
                  _const_spec((D_MODEL, D_MODEL)),
                  _const_spec((1, D_MODEL)),
                  _const_spec((D_MODEL, D_FF)), _const_spec((D_MODEL, D_FF)),
                  _const_spec((D_FF, D_MODEL)),
                  _const_spec((1, D_MODEL))],
        out_specs=row_spec(D_MODEL),
        out_shape=jax.ShapeDtypeStruct((n_tok, D_MODEL), jnp.float32),
        compiler_params=pltpu.CompilerParams(
            dimension_semantics=("arbitrary",), vmem_limit_bytes=VMEM_LIMIT),
        name="back_outproj_ffn2",
    )(x1, mod3, gated, attn, wout, n2, wg, wu, wd, nf)


def kernel(x_prompt, x_sample, c_prompt, c_sample, ada_w, ada_b, ffn1_norm, ffn1_w_gate, ffn1_w_up, ffn1_w_down, mix_norm, w_in, sgu_norm, sgu_w, sgu_b, w_out, ffn2_norm, ffn2_w_gate, ffn2_w_up, ffn2_w_down, final_norm):
    bf = lambda w: w.astype(jnp.bfloat16)
    row = lambda g: g.reshape(1, -1)
    assert ada_w.shape[0] == 1, "single-layer stack"

    c_all = jnp.concatenate([c_prompt, c_sample], axis=0)
    mod3 = _modulation(c_all, ada_w[0], ada_b[0]).reshape(-1, 3 * N_SUBLAYERS, D_MODEL)

    wg1, wu1, wd1 = bf(ffn1_w_gate[0]), bf(ffn1_w_up[0]), bf(ffn1_w_down[0])
    wg2, wu2, wd2 = bf(ffn2_w_gate[0]), bf(ffn2_w_up[0]), bf(ffn2_w_down[0])
    win, wout, sw = bf(w_in[0]), bf(w_out[0]), bf(sgu_w[0])
    sb_full = jnp.repeat(sgu_b[0].T, SGU_GROUP_DIM, axis=1)

    def encode(x, batch_offset):
        b, s, _ = x.shape
        tiles_per_batch = s // TOKEN_TILE
        x2d = x.reshape(b * s, D_MODEL)
        x1, gated, q1, k1, v1, q4, k4, v4, q16, k16, v16 = _front(
            x2d, mod3, batch_offset, tiles_per_batch, row(ffn1_norm[0]), wg1, wu1, wd1,
            row(mix_norm[0]), win, row(sgu_norm[0]), sw, sb_full)
        per_batch = lambda t: t.reshape(b, t.shape[0] // b, t.shape[1])
        state = _attention(per_batch(q16), per_batch(k16), per_batch(v16), None,
                           dil=16, n_res=RES_STEP, tq=Q_BLOCK, final=False)
        state = _attention(per_batch(q4), per_batch(k4), per_batch(v4), state,
                           dil=4, n_res=RES_STEP, tq=Q_BLOCK, final=False)
        attn = _attention(per_batch(q1), per_batch(k1), per_batch(v1), state,
                          dil=1, n_res=1, tq=4 * Q_BLOCK, final=True)
        y = _back(x1, mod3, batch_offset, tiles_per_batch, gated, attn.reshape(b * s, ATTN_WIDTH),
                  wout, row(ffn2_norm[0]), wg2, wu2, wd2, row(final_norm))
        return y.reshape(b, s, D_MODEL)

    y_prompt = encode(x_prompt, 0)
    y_sample = encode(x_sample, x_prompt.shape[0])
    return (y_prompt, y_sample)
```

```python
import functools
import math

import jax
import jax.numpy as jnp
from jax import lax
from jax.experimental import pallas as pl
from jax.experimental.pallas import tpu as pltpu

D_MODEL = 1024
ATTN_HEADS = 8
HEAD_DIM = 64
ATTN_WIDTH = ATTN_HEADS * HEAD_DIM
DILATED_PATTERNS = ((128, 1), (512, 4), (2048, 16))
SGU_WIDTH = D_MODEL - ATTN_WIDTH
SGU_GROUPS = 4
SGU_GROUP_DIM = SGU_WIDTH // SGU_GROUPS
SGU_CHUNK = 128
IN_WIDTH = 3 * ATTN_WIDTH + 2 * SGU_WIDTH
D_FF = 2816
FFN_RESID = 0.5
N_SUBLAYERS = 3
EPS = 1e-6
NEG = -1e30
LOG2E = math.log2(math.e)

V7X_LANES = 128
V7X_VMEM_BYTES = 64 * 1024 * 1024
VMEM_LIMIT = V7X_VMEM_BYTES - 8 * 1024 * 1024

TOKEN_TILE = 512
FF_CHUNKS = ((0, 1024), (1024, 1024), (2048, 768))
Q_BLOCK = 128
HALO = 64
K_BLOCK = Q_BLOCK + 2 * HALO
HEAD_PAIR = 2 * HEAD_DIM
N_PAIRS = ATTN_HEADS // 2
QKV_SLABS = 3 * ATTN_WIDTH // V7X_LANES
RES_STEP = 4
SCORE_LOOKAHEAD = 2

assert all(w // (2 * d) == HALO for w, d in DILATED_PATTERNS)
assert [d for _, d in DILATED_PATTERNS] == [1, RES_STEP, RES_STEP * RES_STEP]
assert HEAD_PAIR == V7X_LANES


def _const_spec(shape):
    zeros = (0,) * len(shape)
    return pl.BlockSpec(shape, lambda *_: zeros, pipeline_mode=pl.Buffered(1))


def _rms(x):
    return x * lax.rsqrt(jnp.mean(x * x, axis=-1, keepdims=True) + EPS)


def _silu(x):
    return x * (1.0 / (1.0 + jnp.exp(-x)))


def _gelu_tanh(x):
    c = math.sqrt(2.0 / math.pi)
    return 0.5 * x * (1.0 + jnp.tanh(c * (x + 0.044715 * (x * x * x))))


def _dot(a, b):
    return jnp.dot(a, b, preferred_element_type=jnp.float32)


def _swiglu(h, wg_ref, wu_ref, wd_ref):
    acc = None
    for start, size in FF_CHUNKS:
        g = _dot(h, wg_ref[:, start:start + size])
        u = _dot(h, wu_ref[:, start:start + size])
        a = (_silu(g) * u).astype(jnp.bfloat16)
        part = _dot(a, wd_ref[start:start + size, :])
        acc = part if acc is None else acc + part
    return acc


def _mod_kernel(c_ref, w_ref, b_ref, o_ref):
    c = c_ref[...]
    o_ref[...] = jnp.dot(_silu(c), w_ref[...], preferred_element_type=jnp.float32,
                         precision=lax.Precision.HIGHEST) + b_ref[...]


def _modulation(c_all, ada_w, ada_b):
    nb = c_all.shape[0]
    width = ada_w.shape[1]
    col = 1536
    return pl.pallas_call(
        _mod_kernel,
        grid=(width // col,),
        in_specs=[pl.BlockSpec((nb, D_MODEL), lambda j: (0, 0)),
                  pl.BlockSpec((D_MODEL, col), lambda j: (0, j)),
                  pl.BlockSpec((1, col), lambda j: (0, j))],
        out_specs=pl.BlockSpec((nb, col), lambda j: (0, j)),
        out_shape=jax.ShapeDtypeStruct((nb, width), jnp.float32),
        name="adaln_mod",
    )(c_all, ada_w, ada_b.reshape(1, width))


def _front_kernel(x_ref, mod_ref, n1_ref, wg_ref, wu_ref, wd_ref, nm_ref, win_ref,
                  sn_ref, sw_ref, sb_ref,
                  x1_ref, gated_ref, q1_ref, k1_ref, v1_ref, q4_ref, k4_ref, v4_ref,
                  q16_ref, k16_ref, v16_ref, zs_ref, zs4_ref):
    x = x_ref[...]
    shift0, scale0, gate0 = mod_ref[0:1, :], mod_ref[1:2, :], mod_ref[2:3, :]
    shift1, scale1 = mod_ref[3:4, :], mod_ref[4:5, :]

    h = (_rms(x) * n1_ref[...]) * (1.0 + scale0) + shift0
    ffn = _swiglu(h.astype(jnp.bfloat16), wg_ref, wu_ref, wd_ref)
    x1 = x + (FFN_RESID * gate0) * ffn
    x1_ref[...] = x1

    h = (_rms(x1) * nm_ref[...]) * (1.0 + scale1) + shift1
    z = _dot(h.astype(jnp.bfloat16), win_ref[...])

    u = _gelu_tanh(z[:, 3 * ATTN_WIDTH:3 * ATTN_WIDTH + SGU_WIDTH])
    sv = _gelu_tanh(z[:, 3 * ATTN_WIDTH + SGU_WIDTH:])
    sv = (_rms(sv) * sn_ref[...]).astype(jnp.bfloat16)
    for n in range(TOKEN_TILE // SGU_CHUNK):
        rows = slice(n * SGU_CHUNK, (n + 1) * SGU_CHUNK)
        mixed = jnp.concatenate(
            [_dot(sw_ref[g], sv[rows, g * SGU_GROUP_DIM:(g + 1) * SGU_GROUP_DIM])
             for g in range(SGU_GROUPS)], axis=1) + sb_ref[...]
        gated_ref[rows, :] = (u[rows, :] * mixed).astype(jnp.bfloat16)

    slabs_per = ATTN_WIDTH // V7X_LANES
    for s in range(QKV_SLABS):
        zslab = z[:, s * V7X_LANES:(s + 1) * V7X_LANES]
        if s < slabs_per:
            zslab = zslab * (HEAD_DIM ** -0.5 * LOG2E)
        zs_ref[s] = zslab
    outs1 = (q1_ref, k1_ref, v1_ref)
    outs4 = (q4_ref, k4_ref, v4_ref)
    outs16 = (q16_ref, k16_ref, v16_ref)
    rows4 = TOKEN_TILE // RES_STEP
    rows16 = rows4 // RES_STEP
    for s in range(QKV_SLABS):
        t, ss = divmod(s, slabs_per)
        lanes = slice(ss * V7X_LANES, (ss + 1) * V7X_LANES)
        outs1[t][:, lanes] = zs_ref[s].astype(jnp.bfloat16)
        for c in range(RES_STEP):
            piece = zs_ref[s, pl.ds(c, rows4, stride=RES_STEP), :]
            zs4_ref[c * QKV_SLABS + s] = piece
            base = c * ATTN_WIDTH + ss * V7X_LANES
            outs4[t][:, base:base + V7X_LANES] = piece.astype(jnp.bfloat16)
    for s in range(QKV_SLABS):
        t, ss = divmod(s, slabs_per)
        for c in range(RES_STEP):
            for a in range(RES_STEP):
                piece = zs4_ref[c * QKV_SLABS + s, pl.ds(a, rows16, stride=RES_STEP), :]
                base = (c * RES_STEP + a) * ATTN_WIDTH + ss * V7X_LANES
                outs16[t][:, base:base + V7X_LANES] = piece.astype(jnp.bfloat16)


def _front(x2d, mod3, batch_offset, tiles_per_batch, n1, wg, wu, wd, nm, win, sn, sw, sb_full):
    n_tok = x2d.shape[0]
    tm = TOKEN_TILE
    row_spec = lambda rows, width: pl.BlockSpec((rows, width), lambda i: (i, 0))
    bf16 = jnp.bfloat16
    views = [(tm, n_tok, ATTN_WIDTH),
             (tm // RES_STEP, n_tok // RES_STEP, RES_STEP * ATTN_WIDTH),
             (tm // RES_STEP ** 2, n_tok // RES_STEP ** 2, RES_STEP ** 2 * ATTN_WIDTH)]
    qkv_specs = [row_spec(r, w) for r, _, w in views for _ in range(3)]
    qkv_shapes = [jax.ShapeDtypeStruct((n, w), bf16) for _, n, w in views for _ in range(3)]
    return pl.pallas_call(
        _front_kernel,
        grid=(n_tok // tm,),
        in_specs=[row_spec(tm, D_MODEL),
                  pl.BlockSpec((None, 3 * N_SUBLAYERS, D_MODEL),
                               lambda i: (batch_offset + i // tiles_per_batch, 0, 0)),
                  _const_spec((1, D_MODEL)),
                  _const_spec((D_MODEL, D_FF)), _const_spec((D_MODEL, D_FF)),
                  _const_spec((D_FF, D_MODEL)),
                  _const_spec((1, D_MODEL)),
                  _const_spec((D_MODEL, IN_WIDTH)),
                  _const_spec((1, SGU_WIDTH)),
                  _const_spec((SGU_GROUPS, SGU_CHUNK, SGU_CHUNK)),
                  _const_spec((SGU_CHUNK, SGU_WIDTH))],
        out_specs=[row_spec(tm, D_MODEL), row_spec(tm, SGU_WIDTH)] + qkv_specs,
        out_shape=[jax.ShapeDtypeStruct((n_tok, D_MODEL), jnp.float32),
                   jax.ShapeDtypeStruct((n_tok, SGU_WIDTH), bf16)] + qkv_shapes,
        scratch_shapes=[pltpu.VMEM((QKV_SLABS, tm, V7X_LANES), jnp.float32),
                        pltpu.VMEM((RES_STEP * QKV_SLABS, tm // RES_STEP, V7X_LANES), jnp.float32)],
        compiler_params=pltpu.CompilerParams(
            dimension_semantics=("arbitrary",), vmem_limit_bytes=VMEM_LIMIT),
        name="front_ffn1_inproj_sgu",
    )(x2d, mod3, n1, wg, wu, wd, nm, win, sn, sw, sb_full)


def _attn_kernel(*refs, dil, tq, seq_len, n_res, has_prev, final):
    refs = list(refs)
    q_ref, kp_ref, km_ref, kn_ref, vp_ref, vm_ref, vn_ref = refs[:7]
    pos = 7
    if has_prev:
        acc_p, m_p, l_p = refs[pos:pos + 3]
        pos += 3
    if final:
        o_ref = refs[pos]
        pos += 1
    else:
        acc_o, m_o, l_o = refs[pos:pos + 3]
        pos += 3
    kwin, vwin, bias_ref = refs[pos:pos + 3]
    pos += 3
    interleave = (not final) and n_res > 1
    if interleave:
        acc_s, m_s, l_s = refs[pos:pos + 3]

    i = pl.program_id(2)
    first_step = (pl.program_id(0) == 0) & (pl.program_id(1) == 0) & (i == 0)

    @pl.when(first_step)
    def _():
        row = lax.broadcasted_iota(jnp.int32, (Q_BLOCK, K_BLOCK), 0)
        col = lax.broadcasted_iota(jnp.int32, (Q_BLOCK, K_BLOCK), 1)
        rel = jnp.abs(col - HALO - row)
        dist = (dil * rel).astype(jnp.float32)
        for h in range(ATTN_HEADS):
            slope = 2.0 ** (-8.0 * (h + 1) / ATTN_HEADS)
            bias_ref[h] = jnp.where(rel <= HALO, (-slope * LOG2E) * dist, NEG)

    for win, prev, main, nxt in ((kwin, kp_ref, km_ref, kn_ref), (vwin, vp_ref, vm_ref, vn_ref)):
        win[0:HALO, :] = prev[...]
        win[HALO:HALO + tq, :] = main[...]
        win[HALO + tq:HALO + tq + HALO, :] = nxt[...]

    lane = lax.broadcasted_iota(jnp.int32, (1, HEAD_PAIR), 1)
    first_head = lane < HEAD_DIM
    col_row = lax.broadcasted_iota(jnp.int32, (1, K_BLOCK), 1)
    n_blocks = tq // Q_BLOCK

    def block_operands(res, j, pair):
        qrows = slice(j * Q_BLOCK, (j + 1) * Q_BLOCK)
        krows = slice(j * Q_BLOCK, j * Q_BLOCK + K_BLOCK)
        cols = slice(res * ATTN_WIDTH + pair * HEAD_PAIR,
                     res * ATTN_WIDTH + (pair + 1) * HEAD_PAIR)
        return qrows, krows, cols

    def head_mask(hh):
        return first_head if hh == 0 else jnp.logical_not(first_head)

    def scores(res, j, pair, hh):
        qrows, krows, cols = block_operands(res, j, pair)
        qp = q_ref[qrows, cols]
        qh = jnp.where(head_mask(hh), qp, jnp.zeros_like(qp))
        return lax.dot_general(qh, kwin[krows, cols], (((1,), (1,)), ((), ())),
                               preferred_element_type=jnp.float32)

    def edge_bias(j):
        if j != 0 and j != n_blocks - 1:
            return None
        kpos = i * tq + (j * Q_BLOCK - HALO) + col_row
        return jnp.where((kpos < 0) | (kpos >= seq_len), NEG, 0.0)

    def softmax_pv(res, j, pair, hh, s):
        _, krows, cols = block_operands(res, j, pair)
        s = s + bias_ref[2 * pair + hh]
        edge = edge_bias(j)
        if edge is not None:
            s = s + edge
        m = jnp.max(s, axis=-1, keepdims=True)
        p = jnp.exp2(s - m)
        vp = vwin[krows, cols]
        vh = jnp.where(head_mask(hh), vp, jnp.ones_like(vp))
        return _dot(p.astype(jnp.bfloat16), vh), m

    def finish_pair(res, j, pair, heads):
        qrows, _, cols = block_operands(res, j, pair)
        (out0, m0), (out1, m1) = heads
        acc = jnp.where(first_head, out0, out1)
        m_pair = jnp.where(first_head, m0, m1)
        l_pair = pltpu.roll(jnp.where(first_head, out1, out0), HEAD_DIM, axis=1)
        if has_prev:
            a_prev = acc_p[qrows, cols].astype(jnp.float32)
            m_prev = m_p[qrows, cols]
            m_new = jnp.maximum(m_prev, m_pair)
            alpha = jnp.exp2(m_prev - m_new)
            beta = jnp.exp2(m_pair - m_new)
            l_pair = alpha * l_p[qrows, cols] + beta * l_pair
            acc = alpha * a_prev + beta * acc
            m_pair = m_new
        if final:
            o_ref[qrows, cols] = (acc / l_pair).astype(o_ref.dtype)
        elif interleave:
            dst = pl.ds(n_res * j * Q_BLOCK + res, Q_BLOCK, stride=n_res)
            acc_s[pair, dst, :] = acc
            m_s[pair, dst, :] = m_pair
            l_s[pair, dst, :] = l_pair
        else:
            acc_o[qrows, cols] = acc.astype(acc_o.dtype)
            m_o[qrows, cols] = m_pair
            l_o[qrows, cols] = l_pair

    items = [(res, j, pair, hh) for res in range(n_res) for j in range(n_blocks)
             for pair in range(N_PAIRS) for hh in range(2)]
    pending, heads = {}, []
    for n in range(len(items) + SCORE_LOOKAHEAD):
        if n < len(items):
            pending[n] = scores(*items[n])
        done = n - SCORE_LOOKAHEAD
        if done >= 0:
            heads.append(softmax_pv(*items[done], pending.pop(done)))
            if len(heads) == 2:
                finish_pair(*items[done][:3], heads)
                heads = []

    if interleave:
        for pair in range(N_PAIRS):
            cols = slice(pair * HEAD_PAIR, (pair + 1) * HEAD_PAIR)
            acc_o[:, cols] = acc_s[pair].astype(acc_o.dtype)
            m_o[:, cols] = m_s[pair]
            l_o[:, cols] = l_s[pair]


def _attention(q, k, v, prev, *, dil, n_res, tq, final):
    b, seq_len, width = q.shape
    n_groups = width // (n_res * ATTN_WIDTH)
    blk_w = n_res * ATTN_WIDTH
    halo_blocks = tq // HALO
    last_halo = seq_len // HALO - 1
    main = pl.BlockSpec((None, tq, blk_w), lambda bi, g, i: (bi, i, g))
    prv = pl.BlockSpec((None, HALO, blk_w),
                       lambda bi, g, i: (bi, jnp.maximum(i * halo_blocks - 1, 0), g))
    nxt = pl.BlockSpec((None, HALO, blk_w),
                       lambda bi, g, i: (bi, jnp.minimum((i + 1) * halo_blocks, last_halo), g))
    out_rows = n_res * tq
    out_spec = pl.BlockSpec((None, out_rows, ATTN_WIDTH), lambda bi, g, i: (bi, i, g))
    out_dims = (b, seq_len * n_res, n_groups * ATTN_WIDTH)

    in_specs = [main, prv, main, nxt, prv, main, nxt]
    args = [q, k, k, k, v, v, v]
    if prev is not None:
        in_specs += [main, main, main]
        args += list(prev)
    scratch = [pltpu.VMEM((tq + 2 * HALO, blk_w), jnp.bfloat16),
               pltpu.VMEM((tq + 2 * HALO, blk_w), jnp.bfloat16),
               pltpu.VMEM((ATTN_HEADS, Q_BLOCK, K_BLOCK), jnp.float32)]
    if final:
        out_specs = [out_spec]
        out_shape = [jax.ShapeDtypeStruct(out_dims, jnp.bfloat16)]
    else:
        out_specs = [out_spec] * 3
        out_shape = [jax.ShapeDtypeStruct(out_dims, jnp.bfloat16),
                     jax.ShapeDtypeStruct(out_dims, jnp.float32),
                     jax.ShapeDtypeStruct(out_dims, jnp.float32)]
        if n_res > 1:
            scratch += [pltpu.VMEM((N_PAIRS, out_rows, HEAD_PAIR), jnp.float32)] * 3
    outs = pl.pallas_call(
        functools.partial(_attn_kernel, dil=dil, tq=tq, seq_len=seq_len, n_res=n_res,
                          has_prev=prev is not None, final=final),
        grid=(b, n_groups, seq_len // tq),
        in_specs=in_specs,
        out_specs=out_specs,
        out_shape=out_shape,
        scratch_shapes=scratch,
        compiler_params=pltpu.CompilerParams(
            dimension_semantics=("arbitrary", "arbitrary", "arbitrary"),
            vmem_limit_bytes=VMEM_LIMIT),
        name=f"band_attn_d{dil}",
    )(*args)
    return outs[0] if final else tuple(outs)


def _back_kernel(x1_ref, mod_ref, gated_ref, attn_ref,
                 wout_ref, n2_ref, wg_ref, wu_ref, wd_ref, nf_ref, y_ref):
    gate1 = mod_ref[5:6, :]
    shift2, scale2, gate2 = mod_ref[6:7, :], mod_ref[7:8, :], mod_ref[8:9, :]

    mix = (_dot(attn_ref[...], wout_ref[0:ATTN_WIDTH, :])
           + _dot(gated_ref[...], wout_ref[ATTN_WIDTH:, :]))
    x2 = x1_ref[...] + gate1 * mix

    h = (_rms(x2) * n2_ref[...]) * (1.0 + scale2) + shift2
    ffn = _swiglu(h.astype(jnp.bfloat16), wg_ref, wu_ref, wd_ref)
    x3 = x2 + (FFN_RESID * gate2) * ffn
    y_ref[...] = _rms(x3) * nf_ref[...]


def _back(x1, mod3, batch_offset, tiles_per_batch, gated, attn, wout, n2, wg, wu, wd, nf):
    n_tok = x1.shape[0]
    tm = TOKEN_TILE
    row_spec = lambda width: pl.BlockSpec((tm, width), lambda i: (i, 0))
    return pl.pallas_call(
        _back_kernel,
        grid=(n_tok // tm,),
        in_specs=[row_spec(D_MODEL),
                  pl.BlockSpec((None, 3 * N_SUBLAYERS, D_MODEL),
                               lambda i: (batch_offset + i // tiles_per_batch, 0, 0)),
                  row_spec(SGU_WIDTH), row_spec(ATTN_WIDTH),
                  _const_spec((D_MODEL, D_MODEL)),
                  _const_spec((1, D_MODEL)),
                  _const_spec((D_MODEL, D_FF)), _const_spec((D_MODEL, D_FF)),
                  _const_spec((D_FF, D_MODEL)),
                  _const_spec((1, D_MODEL))],
        out_specs=row_spec(D_MODEL),
        out_shape=jax.ShapeDtypeStruct((n_tok, D_MODEL), jnp.float32),
        compiler_params=pltpu.CompilerParams(
            dimension_semantics=("arbitrary",), vmem_limit_bytes=VMEM_LIMIT),
        name="back_outproj_ffn2",
    )(x1, mod3, gated, attn, wout, n2, wg, wu, wd, nf)


def kernel(x_prompt, x_sample, c_prompt, c_sample, ada_w, ada_b, ffn1_norm, ffn1_w_gate, ffn1_w_up, ffn1_w_down, mix_norm, w_in, sgu_norm, sgu_w, sgu_b, w_out, ffn2_norm, ffn2_w_gate, ffn2_w_up, ffn2_w_down, final_norm):
    bf = lambda w: w.astype(jnp.bfloat16)
    row = lambda g: g.reshape(1, -1)
    assert ada_w.shape[0] == 1, "single-layer stack"

    c_all = jnp.concatenate([c_prompt, c_sample], axis=0)
    mod3 = _modulation(c_all, ada_w[0], ada_b[0]).reshape(-1, 3 * N_SUBLAYERS, D_MODEL)

    wg1, wu1, wd1 = bf(ffn1_w_gate[0]), bf(ffn1_w_up[0]), bf(ffn1_w_down[0])
    wg2, wu2, wd2 = bf(ffn2_w_gate[0]), bf(ffn2_w_up[0]), bf(ffn2_w_down[0])
    win, wout, sw = bf(w_in[0]), bf(w_out[0]), bf(sgu_w[0])
    sb_full = jnp.repeat(sgu_b[0].T, SGU_GROUP_DIM, axis=1)

    def encode(x, batch_offset):
        b, s, _ = x.shape
        tiles_per_batch = s // TOKEN_TILE
        x2d = x.reshape(b * s, D_MODEL)
        x1, gated, q1, k1, v1, q4, k4, v4, q16, k16, v16 = _front(
            x2d, mod3, batch_offset, tiles_per_batch, row(ffn1_norm[0]), wg1, wu1, wd1,
            row(mix_norm[0]), win, row(sgu_norm[0]), sw, sb_full)
        per_batch = lambda t: t.reshape(b, t.shape[0] // b, t.shape[1])
        state = _attention(per_batch(q16), per_batch(k16), per_batch(v16), None,
                           dil=16, n_res=RES_STEP, tq=Q_BLOCK, final=False)
        state = _attention(per_batch(q4), per_batch(k4), per_batch(v4), state,
                           dil=4, n_res=RES_STEP, tq=Q_BLOCK, final=False)
        attn = _attention(per_batch(q1), per_batch(k1), per_batch(v1), state,
                          dil=1, n_res=1, tq=4 * Q_BLOCK, final=True)
        y = _back(x1, mod3, batch_offset, tiles_per_batch, gated, attn.reshape(b * s, ATTN_WIDTH),
                  wout, row(ffn2_norm[0]), wg2, wu2, wd2, row(final_norm))
        return y.reshape(b, s, D_MODEL)

    y_prompt = encode(x_prompt, 0)
    y_sample = encode(x_sample, x_prompt.shape[0])
    return (y_prompt, y_sample)
```

```python
import functools
import math

import jax
import jax.numpy as jnp
from jax import lax
from jax.experimental import pallas as pl
from jax.experimental.pallas import tpu as pltpu

D_MODEL = 1024
ATTN_HEADS = 8
HEAD_DIM = 64
ATTN_WIDTH = ATTN_HEADS * HEAD_DIM
DILATED_PATTERNS = ((128, 1), (512, 4), (2048, 16))
SGU_WIDTH = D_MODEL - ATTN_WIDTH
SGU_GROUPS = 4
SGU_GROUP_DIM = SGU_WIDTH // SGU_GROUPS
SGU_CHUNK = 128
IN_WIDTH = 3 * ATTN_WIDTH + 2 * SGU_WIDTH
D_FF = 2816
FFN_RESID = 0.5
N_SUBLAYERS = 3
EPS = 1e-6
NEG = -1e30
LOG2E = math.log2(math.e)

V7X_LANES = 128
V7X_VMEM_BYTES = 64 * 1024 * 1024
VMEM_LIMIT = V7X_VMEM_BYTES - 8 * 1024 * 1024

TOKEN_TILE = 512
FF_CHUNKS = ((0, 1024), (1024, 1024), (2048, 768))
Q_BLOCK = 128
HALO = 64
K_BLOCK = Q_BLOCK + 2 * HALO
HEAD_PAIR = 2 * HEAD_DIM
N_PAIRS = ATTN_HEADS // 2
QKV_SLABS = 3 * ATTN_WIDTH // V7X_LANES
RES_STEP = 4

assert all(w // (2 * d) == HALO for w, d in DILATED_PATTERNS)
assert [d for _, d in DILATED_PATTERNS] == [1, RES_STEP, RES_STEP * RES_STEP]
assert HEAD_PAIR == V7X_LANES


def _const_spec(shape):
    zeros = (0,) * len(shape)
    return pl.BlockSpec(shape, lambda *_: zeros, pipeline_mode=pl.Buffered(1))


def _rms(x):
    return x * lax.rsqrt(jnp.mean(x * x, axis=-1, keepdims=True) + EPS)


def _silu(x):
    return x * (1.0 / (1.0 + jnp.exp(-x)))


def _gelu_tanh(x):
    c = math.sqrt(2.0 / math.pi)
    return 0.5 * x * (1.0 + jnp.tanh(c * (x + 0.044715 * (x * x * x))))


def _dot(a, b):
    return jnp.dot(a, b, preferred_element_type=jnp.float32)


def _swiglu(h, wg_ref, wu_ref, wd_ref):
    acc = None
    for start, size in FF_CHUNKS:
        g = _dot(h, wg_ref[:, start:start + size])
        u = _dot(h, wu_ref[:, start:start + size])
        a = (_silu(g) * u).astype(jnp.bfloat16)
        part = _dot(a, wd_ref[start:start + size, :])
        acc = part if acc is None else acc + part
    return acc


def _mod_kernel(c_ref, w_ref, b_ref, o_ref):
    c = c_ref[...]
    o_ref[...] = jnp.dot(_silu(c), w_ref[...], preferred_element_type=jnp.float32,
                         precision=lax.Precision.HIGHEST) + b_ref[...]


def _modulation(c_all, ada_w, ada_b):
    nb = c_all.shape[0]
    width = ada_w.shape[1]
    col = 1536
    return pl.pallas_call(
        _mod_kernel,
        grid=(width // col,),
        in_specs=[pl.BlockSpec((nb, D_MODEL), lambda j: (0, 0)),
                  pl.BlockSpec((D_MODEL, col), lambda j: (0, j)),
                  pl.BlockSpec((1, col), lambda j: (0, j))],
        out_specs=pl.BlockSpec((nb, col), lambda j: (0, j)),
        out_shape=jax.ShapeDtypeStruct((nb, width), jnp.float32),
        name="adaln_mod",
    )(c_all, ada_w, ada_b.reshape(1, width))


def _front_kernel(x_ref, mod_ref, n1_ref, wg_ref, wu_ref, wd_ref, nm_ref, win_ref,
                  sn_ref, sw_ref, sb_ref,
                  x1_ref, gated_ref, q1_ref, k1_ref, v1_ref, q4_ref, k4_ref, v4_ref,
                  q16_ref, k16_ref, v16_ref, zs_ref, zs4_ref):
    x = x_ref[...]
    shift0, scale0, gate0 = mod_ref[0:1, :], mod_ref[1:2, :], mod_ref[2:3, :]
    shift1, scale1 = mod_ref[3:4, :], mod_ref[4:5, :]

    h = (_rms(x) * n1_ref[...]) * (1.0 + scale0) + shift0
    ffn = _swiglu(h.astype(jnp.bfloat16), wg_ref, wu_ref, wd_ref)
    x1 = x + (FFN_RESID * gate0) * ffn
    x1_ref[...] = x1

    h = (_rms(x1) * nm_ref[...]) * (1.0 + scale1) + shift1
    z = _dot(h.astype(jnp.bfloat16), win_ref[...])

    u = _gelu_tanh(z[:, 3 * ATTN_WIDTH:3 * ATTN_WIDTH + SGU_WIDTH])
    sv = _gelu_tanh(z[:, 3 * ATTN_WIDTH + SGU_WIDTH:])
    sv = (_rms(sv) * sn_ref[...]).astype(jnp.bfloat16)
    for n in range(TOKEN_TILE // SGU_CHUNK):
        rows = slice(n * SGU_CHUNK, (n + 1) * SGU_CHUNK)
        mixed = jnp.concatenate(
            [_dot(sw_ref[g], sv[rows, g * SGU_GROUP_DIM:(g + 1) * SGU_GROUP_DIM])
             for g in range(SGU_GROUPS)], axis=1) + sb_ref[...]
        gated_ref[rows, :] = (u[rows, :] * mixed).astype(jnp.bfloat16)

    slabs_per = ATTN_WIDTH // V7X_LANES
    for s in range(QKV_SLABS):
        zslab = z[:, s * V7X_LANES:(s + 1) * V7X_LANES]
        if s < slabs_per:
            zslab = zslab * (HEAD_DIM ** -0.5 * LOG2E)
        zs_ref[s] = zslab
    outs1 = (q1_ref, k1_ref, v1_ref)
    outs4 = (q4_ref, k4_ref, v4_ref)
    outs16 = (q16_ref, k16_ref, v16_ref)
    rows4 = TOKEN_TILE // RES_STEP
    rows16 = rows4 // RES_STEP
    for s in range(QKV_SLABS):
        t, ss = divmod(s, slabs_per)
        lanes = slice(ss * V7X_LANES, (ss + 1) * V7X_LANES)
        outs1[t][:, lanes] = zs_ref[s].astype(jnp.bfloat16)
        for c in range(RES_STEP):
            piece = zs_ref[s, pl.ds(c, rows4, stride=RES_STEP), :]
            zs4_ref[c * QKV_SLABS + s] = piece
            base = c * ATTN_WIDTH + ss * V7X_LANES
            outs4[t][:, base:base + V7X_LANES] = piece.astype(jnp.bfloat16)
    for s in range(QKV_SLABS):
        t, ss = divmod(s, slabs_per)
        for c in range(RES_STEP):
            for a in range(RES_STEP):
                piece = zs4_ref[c * QKV_SLABS + s, pl.ds(a, rows16, stride=RES_STEP), :]
                base = (c * RES_STEP + a) * ATTN_WIDTH + ss * V7X_LANES
                outs16[t][:, base:base + V7X_LANES] = piece.astype(jnp.bfloat16)


def _front(x2d, mod3, batch_offset, tiles_per_batch, n1, wg, wu, wd, nm, win, sn, sw, sb_full):
    n_tok = x2d.shape[0]
    tm = TOKEN_TILE
    row_spec = lambda rows, width: pl.BlockSpec((rows, width), lambda i: (i, 0))
    bf16 = jnp.bfloat16
    views = [(tm, n_tok, ATTN_WIDTH),
             (tm // RES_STEP, n_tok // RES_STEP, RES_STEP * ATTN_WIDTH),
             (tm // RES_STEP ** 2, n_tok // RES_STEP ** 2, RES_STEP ** 2 * ATTN_WIDTH)]
    qkv_specs = [row_spec(r, w) for r, _, w in views for _ in range(3)]
    qkv_shapes = [jax.ShapeDtypeStruct((n, w), bf16) for _, n, w in views for _ in range(3)]
    return pl.pallas_call(
        _front_kernel,
        grid=(n_tok // tm,),
        in_specs=[row_spec(tm, D_MODEL),
                  pl.BlockSpec((None, 3 * N_SUBLAYERS, D_MODEL),
                               lambda i: (batch_offset + i // tiles_per_batch, 0, 0)),
                  _const_spec((1, D_MODEL)),
                  _const_spec((D_MODEL, D_FF)), _const_spec((D_MODEL, D_FF)),
                  _const_spec((D_FF, D_MODEL)),
                  _const_spec((1, D_MODEL)),
                  _const_spec((D_MODEL, IN_WIDTH)),
                  _const_spec((1, SGU_WIDTH)),
                  _const_spec((SGU_GROUPS, SGU_CHUNK, SGU_CHUNK)),
                  _const_spec((SGU_CHUNK, SGU_WIDTH))],
        out_specs=[row_spec(tm, D_MODEL), row_spec(tm, SGU_WIDTH)] + qkv_specs,
        out_shape=[jax.ShapeDtypeStruct((n_tok, D_MODEL), jnp.float32),
                   jax.ShapeDtypeStruct((n_tok, SGU_WIDTH), bf16)] + qkv_shapes,
        scratch_shapes=[pltpu.VMEM((QKV_SLABS, tm, V7X_LANES), jnp.float32),
                        pltpu.VMEM((RES_STEP * QKV_SLABS, tm // RES_STEP, V7X_LANES), jnp.float32)],
        compiler_params=pltpu.CompilerParams(
            dimension_semantics=("arbitrary",), vmem_limit_bytes=VMEM_LIMIT),
        name="front_ffn1_inproj_sgu",
    )(x2d, mod3, n1, wg, wu, wd, nm, win, sn, sw, sb_full)


def _attn_kernel(q_ref, kp_ref, km_ref, kn_ref, vp_ref, vm_ref, vn_ref, o_ref, lse_ref,
                 bias_ref, *scratch, dil, tq, seq_len, n_res):
    interleave = n_res > 1
    n_blocks = tq // Q_BLOCK
    if interleave:
        o_s, lse_s = scratch

    i = pl.program_id(2)
    first_step = (pl.program_id(0) == 0) & (pl.program_id(1) == 0) & (i == 0)
    pair_w = 2 * Q_BLOCK

    @pl.when(first_step)
    def _():
        key = lax.broadcasted_iota(jnp.int32, (K_BLOCK, pair_w), 0)
        col = lax.broadcasted_iota(jnp.int32, (K_BLOCK, pair_w), 1)
        query = jnp.where(col < Q_BLOCK, col, col - Q_BLOCK)
        rel = jnp.abs(key - HALO - query)
        dist = (dil * rel).astype(jnp.float32)
        for pair in range(N_PAIRS):
            slope_a = 2.0 ** (-8.0 * (2 * pair + 1) / ATTN_HEADS)
            slope_b = 2.0 ** (-8.0 * (2 * pair + 2) / ATTN_HEADS)
            slope = jnp.where(col < Q_BLOCK, slope_a * LOG2E, slope_b * LOG2E)
            bias_ref[pair] = jnp.where(rel <= HALO, -slope * dist, NEG)

    lane = lax.broadcasted_iota(jnp.int32, (1, HEAD_PAIR), 1)
    first_head = lane < HEAD_DIM
    key_row = lax.broadcasted_iota(jnp.int32, (K_BLOCK, pair_w), 0)

    def edge_bias(j):
        if j != 0 and j != n_blocks - 1:
            return None
        kpos = i * tq + (j * Q_BLOCK - HALO) + key_row
        return jnp.where((kpos < 0) | (kpos >= seq_len), NEG, 0.0)

    edges = {j: edge_bias(j) for j in range(n_blocks)}

    def block_operands(res, j, pair):
        qrows = slice(j * Q_BLOCK, (j + 1) * Q_BLOCK)
        cols = slice(res * ATTN_WIDTH + pair * HEAD_PAIR,
                     res * ATTN_WIDTH + (pair + 1) * HEAD_PAIR)
        return qrows, cols

    def window(prev, main, nxt, j, cols):
        lo, hi = j * Q_BLOCK - HALO, (j + 1) * Q_BLOCK + HALO
        parts = []
        if lo < 0:
            parts.append(prev[:, cols])
        parts.append(main[max(lo, 0):min(hi, tq), cols])
        if hi > tq:
            parts.append(nxt[:, cols])
        return parts[0] if len(parts) == 1 else jnp.concatenate(parts, axis=0)

    def scores(res, j, pair):
        qrows, cols = block_operands(res, j, pair)
        qp = q_ref[qrows, cols]
        zero = jnp.zeros_like(qp)
        qstack = jnp.concatenate([jnp.where(first_head, qp, zero), jnp.where(first_head, zero, qp)], axis=0)
        kp = window(kp_ref, km_ref, kn_ref, j, cols)
        return lax.dot_general(kp, qstack, (((1,), (1,)), ((), ())),
                               preferred_element_type=jnp.float32)

    def softmax_pv(res, j, pair, s):
        qrows, cols = block_operands(res, j, pair)
        s = s + bias_ref[pair]
        if edges[j] is not None:
            s = s + edges[j]
        m = jnp.max(s, axis=0, keepdims=True)
        p = jnp.exp2(s - m).astype(jnp.bfloat16)
        vp = window(vp_ref, vm_ref, vn_ref, j, cols)
        one = jnp.ones_like(vp)
        out_a = lax.dot_general(jnp.where(first_head, vp, one), p[:, :Q_BLOCK],
                                (((0,), (0,)), ((), ())), preferred_element_type=jnp.float32)
        out_b = lax.dot_general(jnp.where(first_head, one, vp), p[:, Q_BLOCK:],
                                (((0,), (0,)), ((), ())), preferred_element_type=jnp.float32)
        o_t = jnp.concatenate([out_a[:HEAD_DIM] / out_a[HEAD_DIM:], out_b[HEAD_DIM:] / out_b[:HEAD_DIM]],
                              axis=0)
        lse_t = jnp.concatenate([m[:, :Q_BLOCK] + jnp.log2(out_a[HEAD_DIM:]),
                                 m[:, Q_BLOCK:] + jnp.log2(out_b[:HEAD_DIM])], axis=0)
        o, lse = o_t.T, lse_t.T
        if interleave:
            dst = pl.ds(n_res * j * Q_BLOCK + res, Q_BLOCK, stride=n_res)
            o_s[pair, dst, :] = o
            lse_s[pair, dst, :] = lse
        else:
            o_ref[qrows, cols] = o.astype(o_ref.dtype)
            lse_ref[qrows, cols] = lse

    for res in range(n_res):
        for j in range(n_blocks):
            for pair in range(N_PAIRS):
                softmax_pv(res, j, pair, scores(res, j, pair))

    if interleave:
        for pair in range(N_PAIRS):
            cols = slice(pair * HEAD_PAIR, (pair + 1) * HEAD_PAIR)
            o_ref[:, cols] = o_s[pair].astype(o_ref.dtype)
            lse_ref[:, cols] = lse_s[pair]


def _attention(q, k, v, *, dil, n_res, tq, o_dtype):
    b, seq_len, width = q.shape
    n_groups = width // (n_res * ATTN_WIDTH)
    blk_w = n_res * ATTN_WIDTH
    halo_blocks = tq // HALO
    last_halo = seq_len // HALO - 1
    main = pl.BlockSpec((None, tq, blk_w), lambda bi, g, i: (bi, i, g))
    prv = pl.BlockSpec((None, HALO, blk_w),
                       lambda bi, g, i: (bi, jnp.maximum(i * halo_blocks - 1, 0), g))
    nxt = pl.BlockSpec((None, HALO, blk_w),
                       lambda bi, g, i: (bi, jnp.minimum((i + 1) * halo_blocks, last_halo), g))
    out_rows = n_res * tq
    out_spec = pl.BlockSpec((None, out_rows, ATTN_WIDTH), lambda bi, g, i: (bi, i, g))
    out_dims = (b, seq_len * n_res, n_groups * ATTN_WIDTH)

    scratch = [pltpu.VMEM((N_PAIRS, K_BLOCK, 2 * Q_BLOCK), jnp.float32)]
    if n_res > 1:
        scratch += [pltpu.VMEM((N_PAIRS, out_rows, HEAD_PAIR), jnp.float32)] * 2
    return pl.pallas_call(
        functools.partial(_attn_kernel, dil=dil, tq=tq, seq_len=seq_len, n_res=n_res),
        grid=(b, n_groups, seq_len // tq),
        in_specs=[main, prv, main, nxt, prv, main, nxt],
        out_specs=[out_spec, out_spec],
        out_shape=[jax.ShapeDtypeStruct(out_dims, o_dtype),
                   jax.ShapeDtypeStruct(out_dims, jnp.float32)],
        scratch_shapes=scratch,
        compiler_params=pltpu.CompilerParams(
            dimension_semantics=("arbitrary", "arbitrary", "arbitrary"),
            vmem_limit_bytes=VMEM_LIMIT),
        name=f"band_attn_d{dil}",
    )(q, k, k, k, v, v, v)


def _back_kernel(x1_ref, mod_ref, gated_ref, o16_ref, l16_ref, o4_ref, l4_ref, o1_ref, l1_ref,
                 wout_ref, n2_ref, wg_ref, wu_ref, wd_ref, nf_ref, y_ref, o16n_ref, l16n_ref):
    gate1 = mod_ref[5:6, :]
    shift2, scale2, gate2 = mod_ref[6:7, :], mod_ref[7:8, :], mod_ref[8:9, :]

    slabs = ATTN_WIDTH // V7X_LANES
    rows4 = TOKEN_TILE // RES_STEP
    for c in range(RES_STEP):
        for sl in range(slabs):
            lanes = slice(c * ATTN_WIDTH + sl * V7X_LANES, c * ATTN_WIDTH + (sl + 1) * V7X_LANES)
            dst = pl.ds(c, rows4, stride=RES_STEP)
            o16n_ref[sl, dst, :] = o16_ref[:, lanes]
            l16n_ref[sl, dst, :] = l16_ref[:, lanes]

    parts = []
    for sl in range(slabs):
        lanes = slice(sl * V7X_LANES, (sl + 1) * V7X_LANES)
        l16, l4, l1 = l16n_ref[sl], l4_ref[:, lanes], l1_ref[:, lanes]
        top = jnp.maximum(jnp.maximum(l16, l4), l1)
        w16, w4, w1 = jnp.exp2(l16 - top), jnp.exp2(l4 - top), jnp.exp2(l1 - top)
        num = (w16 * o16n_ref[sl] + w4 * o4_ref[:, lanes].astype(jnp.float32)
               + w1 * o1_ref[:, lanes].astype(jnp.float32))
        parts.append((num / (w16 + w4 + w1)).astype(jnp.bfloat16))
    attn = jnp.concatenate(parts, axis=1)

    mix = (_dot(attn, wout_ref[0:ATTN_WIDTH, :])
           + _dot(gated_ref[...], wout_ref[ATTN_WIDTH:, :]))
    x2 = x1_ref[...] + gate1 * mix

    h = (_rms(x2) * n2_ref[...]) * (1.0 + scale2) + shift2
    ffn = _swiglu(h.astype(jnp.bfloat16), wg_ref, wu_ref, wd_ref)
    x3 = x2 + (FFN_RESID * gate2) * ffn
    y_ref[...] = _rms(x3) * nf_ref[...]


def _back(x1, mod3, batch_offset, tiles_per_batch, gated, res16, res4, res1,
          wout, n2, wg, wu, wd, nf):
    n_tok = x1.shape[0]
    tm = TOKEN_TILE
    row_spec = lambda width: pl.BlockSpec((tm, width), lambda i: (i, 0))
    view4_spec = pl.BlockSpec((tm // RES_STEP, RES_STEP * ATTN_WIDTH), lambda i: (i, 0))
    slabs = ATTN_WIDTH // V7X_LANES
    return pl.pallas_call(
        _back_kernel,
        grid=(n_tok // tm,),
        in_specs=[row_spec(D_MODEL),
                  pl.BlockSpec((None, 3 * N_SUBLAYERS, D_MODEL),
                               lambda i: (batch_offset + i // tiles_per_batch, 0, 0)),
                  row_spec(SGU_WIDTH),
                  view4_spec, view4_spec,
                  row_spec(ATTN_WIDTH), row_spec(ATTN_WIDTH),
                  row_spec(ATTN_WIDTH), row_spec(ATTN_WIDTH),
                  _const_spec((D_MODEL, D_MODEL)),
                  _const_spec((1, D_MODEL)),
                  _const_spec((D_MODEL, D_FF)), _const_spec((D_MODEL, D_FF)),
                  _const_spec((D_FF, D_MODEL)),
                  _const_spec((1, D_MODEL))],
        out_specs=row_spec(D_MODEL),
        out_shape=jax.ShapeDtypeStruct((n_tok, D_MODEL), jnp.float32),
        scratch_shapes=[pltpu.VMEM((slabs, tm, V7X_LANES), jnp.float32)] * 2,
        compiler_params=pltpu.CompilerParams(
            dimension_semantics=("arbitrary",), vmem_limit_bytes=VMEM_LIMIT),
        name="back_merge_outproj_ffn2",
    )(x1, mod3, gated, *res16, *res4, *res1, wout, n2, wg, wu, wd, nf)


def kernel(x_prompt, x_sample, c_prompt, c_sample, ada_w, ada_b, ffn1_norm, ffn1_w_gate, ffn1_w_up, ffn1_w_down, mix_norm, w_in, sgu_norm, sgu_w, sgu_b, w_out, ffn2_norm, ffn2_w_gate, ffn2_w_up, ffn2_w_down, final_norm):
    bf = lambda w: w.astype(jnp.bfloat16)
    row = lambda g: g.reshape(1, -1)
    assert ada_w.shape[0] == 1, "single-layer stack"

    c_all = jnp.concatenate([c_prompt, c_sample], axis=0)
    mod3 = _modulation(c_all, ada_w[0], ada_b[0]).reshape(-1, 3 * N_SUBLAYERS, D_MODEL)

    wg1, wu1, wd1 = bf(ffn1_w_gate[0]), bf(ffn1_w_up[0]), bf(ffn1_w_down[0])
    wg2, wu2, wd2 = bf(ffn2_w_gate[0]), bf(ffn2_w_up[0]), bf(ffn2_w_down[0])
    win, wout, sw = bf(w_in[0]), bf(w_out[0]), bf(sgu_w[0])
    sb_full = jnp.repeat(sgu_b[0].T, SGU_GROUP_DIM, axis=1)

    def encode(x, batch_offset):
        b, s, _ = x.shape
        tiles_per_batch = s // TOKEN_TILE
        x2d = x.reshape(b * s, D_MODEL)
        x1, gated, q1, k1, v1, q4, k4, v4, q16, k16, v16 = _front(
            x2d, mod3, batch_offset, tiles_per_batch, row(ffn1_norm[0]), wg1, wu1, wd1,
            row(mix_norm[0]), win, row(sgu_norm[0]), sw, sb_full)
        per_batch = lambda t: t.reshape(b, t.shape[0] // b, t.shape[1])
        res16 = _attention(per_batch(q16), per_batch(k16), per_batch(v16),
                           dil=16, n_res=RES_STEP, tq=Q_BLOCK, o_dtype=jnp.float32)
        res4 = _attention(per_batch(q4), per_batch(k4), per_batch(v4),
                          dil=4, n_res=RES_STEP, tq=Q_BLOCK, o_dtype=jnp.bfloat16)
        res1 = _attention(per_batch(q1), per_batch(k1), per_batch(v1),
                          dil=1, n_res=1, tq=4 * Q_BLOCK, o_dtype=jnp.bfloat16)
        flat = lambda t: t.reshape(t.shape[0] * t.shape[1], t.shape[2])
        y = _back(x1, mod3, batch_offset, tiles_per_batch, gated,
                  [flat(t) for t in res16], [flat(t) for t in res4], [flat(t) for t in res1],
                  wout, row(ffn2_norm[0]), wg2, wu2, wd2, row(final_norm))
        return y.reshape(b, s, D_MODEL)

    y_prompt = encode(x_prompt, 0)
    y_sample = encode(x_sample, x_prompt.shape[0])
    return (y_prompt, y_sample)
```

```python
import functools
import math

import jax
import jax.numpy as jnp
from jax import lax
from jax.experimental import pallas as pl
from jax.experimental.pallas import tpu as pltpu

D_MODEL = 1024
ATTN_HEADS = 8
HEAD_DIM = 64
ATTN_WIDTH = ATTN_HEADS * HEAD_DIM
DILATED_PATTERNS = ((128, 1), (512, 4), (2048, 16))
SGU_WIDTH = D_MODEL - ATTN_WIDTH
SGU_GROUPS = 4
SGU_GROUP_DIM = SGU_WIDTH // SGU_GROUPS
SGU_CHUNK = 128
IN_WIDTH = 3 * ATTN_WIDTH + 2 * SGU_WIDTH
D_FF = 2816
FFN_RESID = 0.5
N_SUBLAYERS = 3
EPS = 1e-6
NEG = -1e30
LOG2E = math.log2(math.e)

V7X_LANES = 128
V7X_SUBLANES = 8
V7X_VMEM_BYTES = 64 * 1024 * 1024
VMEM_LIMIT = V7X_VMEM_BYTES - 8 * 1024 * 1024

TOKEN_TILE = 512
FF_CHUNKS = ((0, 1024), (1024, 1024), (2048, 768))
Q_BLOCK = 128
HALO = 64
K_BLOCK = Q_BLOCK + 2 * HALO
HEAD_PAIR = 2 * HEAD_DIM
N_PAIRS = ATTN_HEADS // 2
QKV_SLABS = 3 * ATTN_WIDTH // V7X_LANES
RES_STEP = 4

assert all(w // (2 * d) == HALO for w, d in DILATED_PATTERNS)
assert [d for _, d in DILATED_PATTERNS] == [1, RES_STEP, RES_STEP * RES_STEP]
assert HEAD_PAIR == V7X_LANES


def _const_spec(shape):
    zeros = (0,) * len(shape)
    return pl.BlockSpec(shape, lambda *_: zeros, pipeline_mode=pl.Buffered(1))


def _rms(x):
    return x * lax.rsqrt(jnp.mean(x * x, axis=-1, keepdims=True) + EPS)


def _silu(x):
    return x * (1.0 / (1.0 + jnp.exp(-x)))


def _gelu_tanh(x):
    c = math.sqrt(2.0 / math.pi)
    return 0.5 * x * (1.0 + jnp.tanh(c * (x + 0.044715 * (x * x * x))))


def _dot(a, b):
    return jnp.dot(a, b, preferred_element_type=jnp.float32)


def _swiglu(h, wg_ref, wu_ref, wd_ref):
    acc = None
    for start, size in FF_CHUNKS:
        g = _dot(h, wg_ref[:, start:start + size])
        u = _dot(h, wu_ref[:, start:start + size])
        a = (_silu(g) * u).astype(jnp.bfloat16)
        part = _dot(a, wd_ref[start:start + size, :])
        acc = part if acc is None else acc + part
    return acc


def _mod_kernel(c_ref, w_ref, b_ref, o_ref):
    c = c_ref[...]
    o_ref[...] = jnp.dot(_silu(c), w_ref[...], preferred_element_type=jnp.float32,
                         precision=lax.Precision.HIGHEST) + b_ref[...]


def _modulation(c_all, ada_w, ada_b):
    nb = c_all.shape[0]
    width = ada_w.shape[1]
    col = 1536
    return pl.pallas_call(
        _mod_kernel,
        grid=(width // col,),
        in_specs=[pl.BlockSpec((nb, D_MODEL), lambda j: (0, 0)),
                  pl.BlockSpec((D_MODEL, col), lambda j: (0, j)),
                  pl.BlockSpec((1, col), lambda j: (0, j))],
        out_specs=pl.BlockSpec((nb, col), lambda j: (0, j)),
        out_shape=jax.ShapeDtypeStruct((nb, width), jnp.float32),
        name="adaln_mod",
    )(c_all, ada_w, ada_b.reshape(1, width))


def _front_kernel(x_ref, mod_ref, n1_ref, wg_ref, wu_ref, wd_ref, nm_ref, win_ref,
                  sn_ref, sw_ref, sb_ref,
                  x1_ref, gated_ref, q1_ref, k1_ref, v1_ref, q4_ref, k4_ref, v4_ref,
                  q16_ref, k16_ref, v16_ref, zs_ref, zs4_ref):
    x = x_ref[...]
    shift0, scale0, gate0 = mod_ref[0:1, :], mod_ref[1:2, :], mod_ref[2:3, :]
    shift1, scale1 = mod_ref[3:4, :], mod_ref[4:5, :]

    h = (_rms(x) * n1_ref[...]) * (1.0 + scale0) + shift0
    ffn = _swiglu(h.astype(jnp.bfloat16), wg_ref, wu_ref, wd_ref)
    x1 = x + (FFN_RESID * gate0) * ffn
    x1_ref[...] = x1

    h = (_rms(x1) * nm_ref[...]) * (1.0 + scale1) + shift1
    z = _dot(h.astype(jnp.bfloat16), win_ref[...])

    u = _gelu_tanh(z[:, 3 * ATTN_WIDTH:3 * ATTN_WIDTH + SGU_WIDTH])
    sv = _gelu_tanh(z[:, 3 * ATTN_WIDTH + SGU_WIDTH:])
    sv = (_rms(sv) * sn_ref[...]).astype(jnp.bfloat16)
    for n in range(TOKEN_TILE // SGU_CHUNK):
        rows = slice(n * SGU_CHUNK, (n + 1) * SGU_CHUNK)
        mixed = jnp.concatenate(
            [_dot(sw_ref[g], sv[rows, g * SGU_GROUP_DIM:(g + 1) * SGU_GROUP_DIM])
             for g in range(SGU_GROUPS)], axis=1) + sb_ref[...]
        gated_ref[rows, :] = (u[rows, :] * mixed).astype(jnp.bfloat16)

    slabs_per = ATTN_WIDTH // V7X_LANES
    for s in range(QKV_SLABS):
        zslab = z[:, s * V7X_LANES:(s + 1) * V7X_LANES]
        if s < slabs_per:
            zslab = zslab * (HEAD_DIM ** -0.5 * LOG2E)
        zs_ref[s] = zslab
    outs1 = (q1_ref, k1_ref, v1_ref)
    outs4 = (q4_ref, k4_ref, v4_ref)
    outs16 = (q16_ref, k16_ref, v16_ref)
    rows4 = TOKEN_TILE // RES_STEP
    rows16 = rows4 // RES_STEP
    for s in range(QKV_SLABS):
        t, ss = divmod(s, slabs_per)
        lanes = slice(ss * V7X_LANES, (ss + 1) * V7X_LANES)
        outs1[t][:, lanes] = zs_ref[s].astype(jnp.bfloat16)
        for c in range(RES_STEP):
            piece = zs_ref[s, pl.ds(c, rows4, stride=RES_STEP), :]
            zs4_ref[c * QKV_SLABS + s] = piece
            base = c * ATTN_WIDTH + ss * V7X_LANES
            outs4[t][:, base:base + V7X_LANES] = piece.astype(jnp.bfloat16)
    for s in range(QKV_SLABS):
        t, ss = divmod(s, slabs_per)
        for c in range(RES_STEP):
            for a in range(RES_STEP):
                piece = zs4_ref[c * QKV_SLABS + s, pl.ds(a, rows16, stride=RES_STEP), :]
                base = (c * RES_STEP + a) * ATTN_WIDTH + ss * V7X_LANES
                outs16[t][:, base:base + V7X_LANES] = piece.astype(jnp.bfloat16)


def _front(x2d, mod3, batch_offset, tiles_per_batch, n1, wg, wu, wd, nm, win, sn, sw, sb_full):
    n_tok = x2d.shape[0]
    tm = TOKEN_TILE
    row_spec = lambda rows, width: pl.BlockSpec((rows, width), lambda i: (i, 0))
    bf16 = jnp.bfloat16
    views = [(tm, n_tok, ATTN_WIDTH),
             (tm // RES_STEP, n_tok // RES_STEP, RES_STEP * ATTN_WIDTH),
             (tm // RES_STEP ** 2, n_tok // RES_STEP ** 2, RES_STEP ** 2 * ATTN_WIDTH)]
    qkv_specs = [row_spec(r, w) for r, _, w in views for _ in range(3)]
    qkv_shapes = [jax.ShapeDtypeStruct((n, w), bf16) for _, n, w in views for _ in range(3)]
    return pl.pallas_call(
        _front_kernel,
        grid=(n_tok // tm,),
        in_specs=[row_spec(tm, D_MODEL),
                  pl.BlockSpec((None, 3 * N_SUBLAYERS, D_MODEL),
                               lambda i: (batch_offset + i // tiles_per_batch, 0, 0)),
                  _const_spec((1, D_MODEL)),
                  _const_spec((D_MODEL, D_FF)), _const_spec((D_MODEL, D_FF)),
                  _const_spec((D_FF, D_MODEL)),
                  _const_spec((1, D_MODEL)),
                  _const_spec((D_MODEL, IN_WIDTH)),
                  _const_spec((1, SGU_WIDTH)),
                  _const_spec((SGU_GROUPS, SGU_CHUNK, SGU_CHUNK)),
                  _const_spec((SGU_CHUNK, SGU_WIDTH))],
        out_specs=[row_spec(tm, D_MODEL), row_spec(tm, SGU_WIDTH)] + qkv_specs,
        out_shape=[jax.ShapeDtypeStruct((n_tok, D_MODEL), jnp.float32),
                   jax.ShapeDtypeStruct((n_tok, SGU_WIDTH), bf16)] + qkv_shapes,
        scratch_shapes=[pltpu.VMEM((QKV_SLABS, tm, V7X_LANES), jnp.float32),
                        pltpu.VMEM((RES_STEP * QKV_SLABS, tm // RES_STEP, V7X_LANES), jnp.float32)],
        compiler_params=pltpu.CompilerParams(
            dimension_semantics=("arbitrary",), vmem_limit_bytes=VMEM_LIMIT),
        name="front_ffn1_inproj_sgu",
    )(x2d, mod3, n1, wg, wu, wd, nm, win, sn, sw, sb_full)


def _attn_kernel(q_ref, kp_ref, km_ref, kn_ref, vp_ref, vm_ref, vn_ref, o_ref, lse_ref,
                 bias_ref, *scratch, dil, tq, seq_len, n_res):
    interleave = n_res > 1
    n_blocks = tq // Q_BLOCK
    if interleave:
        o_s, lse_s = scratch

    i = pl.program_id(2)
    first_step = (pl.program_id(0) == 0) & (pl.program_id(1) == 0) & (i == 0)
    pair_w = 2 * Q_BLOCK

    @pl.when(first_step)
    def _():
        key = lax.broadcasted_iota(jnp.int32, (K_BLOCK, pair_w), 0)
        col = lax.broadcasted_iota(jnp.int32, (K_BLOCK, pair_w), 1)
        query = jnp.where(col < Q_BLOCK, col, col - Q_BLOCK)
        rel = jnp.abs(key - HALO - query)
        dist = (dil * rel).astype(jnp.float32)
        before_start = key < HALO
        past_end = key >= HALO + Q_BLOCK
        for pair in range(N_PAIRS):
            slope_a = 2.0 ** (-8.0 * (2 * pair + 1) / ATTN_HEADS)
            slope_b = 2.0 ** (-8.0 * (2 * pair + 2) / ATTN_HEADS)
            slope = jnp.where(col < Q_BLOCK, slope_a * LOG2E, slope_b * LOG2E)
            plain = jnp.where(rel <= HALO, -slope * dist, NEG)
            bias_ref[0, pair] = plain
            bias_ref[1, pair] = jnp.where(before_start, NEG, plain)
            bias_ref[2, pair] = jnp.where(past_end, NEG, plain)
            bias_ref[3, pair] = jnp.where(before_start | past_end, NEG, plain)

    lane = lax.broadcasted_iota(jnp.int32, (1, HEAD_PAIR), 1)
    first_head = lane < HEAD_DIM
    last_tile = seq_len // tq - 1

    def bias_variant(j):
        variant = 0
        if j == 0:
            variant = variant + jnp.where(i == 0, 1, 0)
        if j == n_blocks - 1:
            variant = variant + jnp.where(i == last_tile, 2, 0)
        return variant

    def block_operands(res, j, pair):
        qrows = slice(j * Q_BLOCK, (j + 1) * Q_BLOCK)
        cols = slice(res * ATTN_WIDTH + pair * HEAD_PAIR,
                     res * ATTN_WIDTH + (pair + 1) * HEAD_PAIR)
        return qrows, cols

    def window(prev, main, nxt, j, cols):
        lo, hi = j * Q_BLOCK - HALO, (j + 1) * Q_BLOCK + HALO
        parts = []
        if lo < 0:
            parts.append(prev[:, cols])
        parts.append(main[max(lo, 0):min(hi, tq), cols])
        if hi > tq:
            parts.append(nxt[:, cols])
        return parts[0] if len(parts) == 1 else jnp.concatenate(parts, axis=0)

    def scores(res, j, pair):
        qrows, cols = block_operands(res, j, pair)
        qp = q_ref[qrows, cols]
        zero = jnp.zeros_like(qp)
        qstack = jnp.concatenate([jnp.where(first_head, qp, zero), jnp.where(first_head, zero, qp)], axis=0)
        kp = window(kp_ref, km_ref, kn_ref, j, cols)
        return lax.dot_general(kp, qstack, (((1,), (1,)), ((), ())),
                               preferred_element_type=jnp.float32)

    def softmax_pv(res, j, pair, s):
        qrows, cols = block_operands(res, j, pair)
        s = s + bias_ref[bias_variant(j), pair]
        m = jnp.max(s, axis=0, keepdims=True)
        p = jnp.exp2(s - m).astype(jnp.bfloat16)
        vp = window(vp_ref, vm_ref, vn_ref, j, cols)
        one = jnp.ones_like(vp)
        out_a = lax.dot_general(jnp.where(first_head, vp, one), p[:, :Q_BLOCK],
                                (((0,), (0,)), ((), ())), preferred_element_type=jnp.float32)
        out_b = lax.dot_general(jnp.where(first_head, one, vp), p[:, Q_BLOCK:],
                                (((0,), (0,)), ((), ())), preferred_element_type=jnp.float32)
        sum_a = out_a[HEAD_DIM:HEAD_DIM + V7X_SUBLANES]
        sum_b = out_b[:V7X_SUBLANES]
        reps = HEAD_DIM // V7X_SUBLANES
        rows = lambda t: jnp.concatenate([t] * reps, axis=0)
        o_t = jnp.concatenate([out_a[:HEAD_DIM] * rows(1.0 / sum_a),
                               out_b[HEAD_DIM:] * rows(1.0 / sum_b)], axis=0)
        lse_t = jnp.concatenate([rows(m[:, :Q_BLOCK] + jnp.log2(sum_a)),
                                 rows(m[:, Q_BLOCK:] + jnp.log2(sum_b))], axis=0)
        o, lse = o_t.T, lse_t.T
        if interleave:
            dst = pl.ds(n_res * j * Q_BLOCK + res, Q_BLOCK, stride=n_res)
            o_s[pair, dst, :] = o
            lse_s[pair, dst, :] = lse
        else:
            o_ref[qrows, cols] = o.astype(o_ref.dtype)
            lse_ref[qrows, cols] = lse

    for res in range(n_res):
        for j in range(n_blocks):
            for pair in range(N_PAIRS):
                softmax_pv(res, j, pair, scores(res, j, pair))

    if interleave:
        for pair in range(N_PAIRS):
            cols = slice(pair * HEAD_PAIR, (pair + 1) * HEAD_PAIR)
            o_ref[:, cols] = o_s[pair].astype(o_ref.dtype)
            lse_ref[:, cols] = lse_s[pair]


def _attention(q, k, v, *, dil, n_res, tq, o_dtype):
    b, seq_len, width = q.shape
    n_groups = width // (n_res * ATTN_WIDTH)
    blk_w = n_res * ATTN_WIDTH
    halo_blocks = tq // HALO
    last_halo = seq_len // HALO - 1
    main = pl.BlockSpec((None, tq, blk_w), lambda bi, g, i: (bi, i, g))
    prv = pl.BlockSpec((None, HALO, blk_w),
                       lambda bi, g, i: (bi, jnp.maximum(i * halo_blocks - 1, 0), g))
    nxt = pl.BlockSpec((None, HALO, blk_w),
                       lambda bi, g, i: (bi, jnp.minimum((i + 1) * halo_blocks, last_halo), g))
    out_rows = n_res * tq
    out_spec = pl.BlockSpec((None, out_rows, ATTN_WIDTH), lambda bi, g, i: (bi, i, g))
    out_dims = (b, seq_len * n_res, n_groups * ATTN_WIDTH)

    scratch = [pltpu.VMEM((4, N_PAIRS, K_BLOCK, 2 * Q_BLOCK), jnp.float32)]
    if n_res > 1:
        scratch += [pltpu.VMEM((N_PAIRS, out_rows, HEAD_PAIR), jnp.float32)] * 2
    return pl.pallas_call(
        functools.partial(_attn_kernel, dil=dil, tq=tq, seq_len=seq_len, n_res=n_res),
        grid=(b, n_groups, seq_len // tq),
        in_specs=[main, prv, main, nxt, prv, main, nxt],
        out_specs=[out_spec, out_spec],
        out_shape=[jax.ShapeDtypeStruct(out_dims, o_dtype),
                   jax.ShapeDtypeStruct(out_dims, jnp.float32)],
        scratch_shapes=scratch,
        compiler_params=pltpu.CompilerParams(
            dimension_semantics=("arbitrary", "arbitrary", "arbitrary"),
            vmem_limit_bytes=VMEM_LIMIT),
        name=f"band_attn_d{dil}",
    )(q, k, k, k, v, v, v)


def _back_kernel(x1_ref, mod_ref, gated_ref, o16_ref, l16_ref, o4_ref, l4_ref, o1_ref, l1_ref,
                 wout_ref, n2_ref, wg_ref, wu_ref, wd_ref, nf_ref, y_ref, o16n_ref, l16n_ref):
    gate1 = mod_ref[5:6, :]
    shift2, scale2, gate2 = mod_ref[6:7, :], mod_ref[7:8, :], mod_ref[8:9, :]

    slabs = ATTN_WIDTH // V7X_LANES
    rows4 = TOKEN_TILE // RES_STEP
    for c in range(RES_STEP):
        for sl in range(slabs):
            lanes = slice(c * ATTN_WIDTH + sl * V7X_LANES, c * ATTN_WIDTH + (sl + 1) * V7X_LANES)
            dst = pl.ds(c, rows4, stride=RES_STEP)
            o16n_ref[sl, dst, :] = o16_ref[:, lanes]
            l16n_ref[sl, dst, :] = l16_ref[:, lanes]

    parts = []
    for sl in range(slabs):
        lanes = slice(sl * V7X_LANES, (sl + 1) * V7X_LANES)
        l16, l4, l1 = l16n_ref[sl], l4_ref[:, lanes], l1_ref[:, lanes]
        top = jnp.maximum(jnp.maximum(l16, l4), l1)
        w16, w4, w1 = jnp.exp2(l16 - top), jnp.exp2(l4 - top), jnp.exp2(l1 - top)
        num = (w16 * o16n_ref[sl] + w4 * o4_ref[:, lanes].astype(jnp.float32)
               + w1 * o1_ref[:, lanes].astype(jnp.float32))
        parts.append((num * (1.0 / (w16 + w4 + w1))).astype(jnp.bfloat16))
    attn = jnp.concatenate(parts, axis=1)

    mix = (_dot(attn, wout_ref[0:ATTN_WIDTH, :])
           + _dot(gated_ref[...], wout_ref[ATTN_WIDTH:, :]))
    x2 = x1_ref[...] + gate1 * mix

    h = (_rms(x2) * n2_ref[...]) * (1.0 + scale2) + shift2
    ffn = _swiglu(h.astype(jnp.bfloat16), wg_ref, wu_ref, wd_ref)
    x3 = x2 + (FFN_RESID * gate2) * ffn
    y_ref[...] = _rms(x3) * nf_ref[...]


def _back(x1, mod3, batch_offset, tiles_per_batch, gated, res16, res4, res1,
          wout, n2, wg, wu, wd, nf):
    n_tok = x1.shape[0]
    tm = TOKEN_TILE
    row_spec = lambda width: pl.BlockSpec((tm, width), lambda i: (i, 0))
    view4_spec = pl.BlockSpec((tm // RES_STEP, RES_STEP * ATTN_WIDTH), lambda i: (i, 0))
    slabs = ATTN_WIDTH // V7X_LANES
    return pl.pallas_call(
        _back_kernel,
        grid=(n_tok // tm,),
        in_specs=[row_spec(D_MODEL),
                  pl.BlockSpec((None, 3 * N_SUBLAYERS, D_MODEL),
                               lambda i: (batch_offset + i // tiles_per_batch, 0, 0)),
                  row_spec(SGU_WIDTH),
                  view4_spec, view4_spec,
                  row_spec(ATTN_WIDTH), row_spec(ATTN_WIDTH),
                  row_spec(ATTN_WIDTH), row_spec(ATTN_WIDTH),
                  _const_spec((D_MODEL, D_MODEL)),
                  _const_spec((1, D_MODEL)),
                  _const_spec((D_MODEL, D_FF)), _const_spec((D_MODEL, D_FF)),
                  _const_spec((D_FF, D_MODEL)),
                  _const_spec((1, D_MODEL))],
        out_specs=row_spec(D_MODEL),
        out_shape=jax.ShapeDtypeStruct((n_tok, D_MODEL), jnp.float32),
        scratch_shapes=[pltpu.VMEM((slabs, tm, V7X_LANES), jnp.float32)] * 2,
        compiler_params=pltpu.CompilerParams(
            dimension_semantics=("arbitrary",), vmem_limit_bytes=VMEM_LIMIT),
        name="back_merge_outproj_ffn2",
    )(x1, mod3, gated, *res16, *res4, *res1, wout, n2, wg, wu, wd, nf)


def kernel(x_prompt, x_sample, c_prompt, c_sample, ada_w, ada_b, ffn1_norm, ffn1_w_gate, ffn1_w_up, ffn1_w_down, mix_norm, w_in, sgu_norm, sgu_w, sgu_b, w_out, ffn2_norm, ffn2_w_gate, ffn2_w_up, ffn2_w_down, final_norm):
    bf = lambda w: w.astype(jnp.bfloat16)
    row = lambda g: g.reshape(1, -1)
    assert ada_w.shape[0] == 1, "single-layer stack"

    c_all = jnp.concatenate([c_prompt, c_sample], axis=0)
    mod3 = _modulation(c_all, ada_w[0], ada_b[0]).reshape(-1, 3 * N_SUBLAYERS, D_MODEL)

    wg1, wu1, wd1 = bf(ffn1_w_gate[0]), bf(ffn1_w_up[0]), bf(ffn1_w_down[0])
    wg2, wu2, wd2 = bf(ffn2_w_gate[0]), bf(ffn2_w_up[0]), bf(ffn2_w_down[0])
    win, wout, sw = bf(w_in[0]), bf(w_out[0]), bf(sgu_w[0])
    sb_full = jnp.repeat(sgu_b[0].T, SGU_GROUP_DIM, axis=1)

    def encode(x, batch_offset):
        b, s, _ = x.shape
        tiles_per_batch = s // TOKEN_TILE
        x2d = x.reshape(b * s, D_MODEL)
        x1, gated, q1, k1, v1, q4, k4, v4, q16, k16, v16 = _front(
            x2d, mod3, batch_offset, tiles_per_batch, row(ffn1_norm[0]), wg1, wu1, wd1,
            row(mix_norm[0]), win, row(sgu_norm[0]), sw, sb_full)
        per_batch = lambda t: t.reshape(b, t.shape[0] // b, t.shape[1])
        res16 = _attention(per_batch(q16), per_batch(k16), per_batch(v16),
                           dil=16, n_res=RES_STEP, tq=2 * Q_BLOCK, o_dtype=jnp.float32)
        res4 = _attention(per_batch(q4), per_batch(k4), per_batch(v4),
                          dil=4, n_res=RES_STEP, tq=2 * Q_BLOCK, o_dtype=jnp.bfloat16)
        res1 = _attention(per_batch(q1), per_batch(k1), per_batch(v1),
                          dil=1, n_res=1, tq=8 * Q_BLOCK, o_dtype=jnp.bfloat16)
        flat = lambda t: t.reshape(t.shape[0] * t.shape[1], t.shape[2])
        y = _back(x1, mod3, batch_offset, tiles_per_batch, gated,
                  [flat(t) for t in res16], [flat(t) for t in res4], [flat(t) for t in res1],
                  wout, row(ffn2_norm[0]), wg2, wu2, wd2, row(final_norm))
        return y.reshape(b, s, D_MODEL)

    y_prompt = encode(x_prompt, 0)
    y_sample = encode(x_sample, x_prompt.shape[0])
    return (y_prompt, y_sample)
```

```python
import functools
import math

import jax
import jax.numpy as jnp
from jax import lax
from jax.experimental import pallas as pl
from jax.experimental.pallas import tpu as pltpu

D_MODEL = 1024
ATTN_HEADS = 8
HEAD_DIM = 64
ATTN_WIDTH = ATTN_HEADS * HEAD_DIM
DILATED_PATTERNS = ((128, 1), (512, 4), (2048, 16))
SGU_WIDTH = D_MODEL - ATTN_WIDTH
SGU_GROUPS = 4
SGU_GROUP_DIM = SGU_WIDTH // SGU_GROUPS
SGU_CHUNK = 128
IN_WIDTH = 3 * ATTN_WIDTH + 2 * SGU_WIDTH
D_FF = 2816
FFN_RESID = 0.5
N_SUBLAYERS = 3
EPS = 1e-6
NEG = -1e30
LOG2E = math.log2(math.e)

V7X_LANES = 128
V7X_SUBLANES = 8
V7X_VMEM_BYTES = 64 * 1024 * 1024
VMEM_LIMIT = V7X_VMEM_BYTES - 8 * 1024 * 1024

TOKEN_TILE = 512
ROW_PARTS = 2
FF_CHUNKS = ((0, 1024), (1024, 1024), (2048, 768))
Q_BLOCK = 128
HALO = 64
K_BLOCK = Q_BLOCK + 2 * HALO
HEAD_PAIR = 2 * HEAD_DIM
N_PAIRS = ATTN_HEADS // 2
QKV_SLABS = 3 * ATTN_WIDTH // V7X_LANES
RES_STEP = 4

assert all(w // (2 * d) == HALO for w, d in DILATED_PATTERNS)
assert [d for _, d in DILATED_PATTERNS] == [1, RES_STEP, RES_STEP * RES_STEP]
assert HEAD_PAIR == V7X_LANES


def _const_spec(shape):
    zeros = (0,) * len(shape)
    return pl.BlockSpec(shape, lambda *_: zeros, pipeline_mode=pl.Buffered(1))


def _rms(x):
    return x * lax.rsqrt(jnp.mean(x * x, axis=-1, keepdims=True) + EPS)


def _silu(x):
    return x * (1.0 / (1.0 + jnp.exp(-x)))


def _gelu_tanh(x):
    c = math.sqrt(2.0 / math.pi)
    return 0.5 * x * (1.0 + jnp.tanh(c * (x + 0.044715 * (x * x * x))))


def _dot(a, b):
    return jnp.dot(a, b, preferred_element_type=jnp.float32)


def _swiglu(h, wg_ref, wu_ref, wd_ref):
    acc = None
    for start, size in FF_CHUNKS:
        g = _dot(h, wg_ref[:, start:start + size])
        u = _dot(h, wu_ref[:, start:start + size])
        a = (_silu(g) * u).astype(jnp.bfloat16)
        part = _dot(a, wd_ref[start:start + size, :])
        acc = part if acc is None else acc + part
    return acc


def _mod_kernel(c_ref, w_ref, b_ref, o_ref):
    c = c_ref[...]
    o_ref[...] = jnp.dot(_silu(c), w_ref[...], preferred_element_type=jnp.float32,
                         precision=lax.Precision.HIGHEST) + b_ref[...]


def _modulation(c_all, ada_w, ada_b):
    nb = c_all.shape[0]
    width = ada_w.shape[1]
    col = 1536
    return pl.pallas_call(
        _mod_kernel,
        grid=(width // col,),
        in_specs=[pl.BlockSpec((nb, D_MODEL), lambda j: (0, 0)),
                  pl.BlockSpec((D_MODEL, col), lambda j: (0, j)),
                  pl.BlockSpec((1, col), lambda j: (0, j))],
        out_specs=pl.BlockSpec((nb, col), lambda j: (0, j)),
        out_shape=jax.ShapeDtypeStruct((nb, width), jnp.float32),
        name="adaln_mod",
    )(c_all, ada_w, ada_b.reshape(1, width))


def _front_kernel(x_ref, mod_ref, n1_ref, wg_ref, wu_ref, wd_ref, nm_ref, win_ref,
                  sn_ref, sw_ref, sb_ref,
                  x1_ref, gated_ref, q1_ref, k1_ref, v1_ref, q4_ref, k4_ref, v4_ref,
                  q16_ref, k16_ref, v16_ref, zs_ref, zs4_ref):
    shift0, scale0, gate0 = mod_ref[0:1, :], mod_ref[1:2, :], mod_ref[2:3, :]
    shift1, scale1 = mod_ref[3:4, :], mod_ref[4:5, :]
    slabs_per = ATTN_WIDTH // V7X_LANES
    outs1 = (q1_ref, k1_ref, v1_ref)
    outs4 = (q4_ref, k4_ref, v4_ref)
    outs16 = (q16_ref, k16_ref, v16_ref)
    part_rows = TOKEN_TILE // ROW_PARTS
    rows4 = part_rows // RES_STEP
    rows16 = rows4 // RES_STEP

    def ffn_input(r0):
        x = x_ref[r0:r0 + part_rows, :]
        return ((_rms(x) * n1_ref[...]) * (1.0 + scale0) + shift0).astype(jnp.bfloat16)

    def mixer_input(r0, ffn):
        x1 = x_ref[r0:r0 + part_rows, :] + (FFN_RESID * gate0) * ffn
        x1_ref[r0:r0 + part_rows, :] = x1
        return ((_rms(x1) * nm_ref[...]) * (1.0 + scale1) + shift1).astype(jnp.bfloat16)

    def gating(r0, z):
        u = _gelu_tanh(z[:, 3 * ATTN_WIDTH:3 * ATTN_WIDTH + SGU_WIDTH])
        sv = _gelu_tanh(z[:, 3 * ATTN_WIDTH + SGU_WIDTH:])
        sv = (_rms(sv) * sn_ref[...]).astype(jnp.bfloat16)
        for n in range(part_rows // SGU_CHUNK):
            rows = slice(n * SGU_CHUNK, (n + 1) * SGU_CHUNK)
            mixed = jnp.concatenate(
                [_dot(sw_ref[g], sv[rows, g * SGU_GROUP_DIM:(g + 1) * SGU_GROUP_DIM])
                 for g in range(SGU_GROUPS)], axis=1) + sb_ref[...]
            gated_ref[r0 + n * SGU_CHUNK:r0 + (n + 1) * SGU_CHUNK, :] = (
                u[rows, :] * mixed).astype(jnp.bfloat16)

    def relayout(part, z):
        r0 = part * part_rows
        for s in range(QKV_SLABS):
            zslab = z[:, s * V7X_LANES:(s + 1) * V7X_LANES]
            if s < slabs_per:
                zslab = zslab * (HEAD_DIM ** -0.5 * LOG2E)
            zs_ref[part, s] = zslab
        for s in range(QKV_SLABS):
            t, ss = divmod(s, slabs_per)
            lanes = slice(ss * V7X_LANES, (ss + 1) * V7X_LANES)
            outs1[t][r0:r0 + part_rows, lanes] = zs_ref[part, s].astype(jnp.bfloat16)
            for c in range(RES_STEP):
                piece = zs_ref[part, s, pl.ds(c, rows4, stride=RES_STEP), :]
                zs4_ref[part, c * QKV_SLABS + s] = piece
                base = c * ATTN_WIDTH + ss * V7X_LANES
                outs4[t][part * rows4:(part + 1) * rows4, base:base + V7X_LANES] = piece.astype(jnp.bfloat16)
        for s in range(QKV_SLABS):
            t, ss = divmod(s, slabs_per)
            for c in range(RES_STEP):
                for a in range(RES_STEP):
                    piece = zs4_ref[part, c * QKV_SLABS + s, pl.ds(a, rows16, stride=RES_STEP), :]
                    base = (c * RES_STEP + a) * ATTN_WIDTH + ss * V7X_LANES
                    outs16[t][part * rows16:(part + 1) * rows16, base:base + V7X_LANES] = (
                        piece.astype(jnp.bfloat16))

    starts = [k * part_rows for k in range(ROW_PARTS)]
    h0 = [ffn_input(r0) for r0 in starts]
    ffn = [_swiglu(h, wg_ref, wu_ref, wd_ref) for h in h0]
    h1 = [mixer_input(r0, f) for r0, f in zip(starts, ffn)]
    z = [_dot(h, win_ref[...]) for h in h1]
    for r0, zp in zip(starts, z):
        gating(r0, zp)
    for part, zp in enumerate(z):
        relayout(part, zp)


def _front(x2d, mod3, batch_offset, tiles_per_batch, n1, wg, wu, wd, nm, win, sn, sw, sb_full):
    n_tok = x2d.shape[0]
    tm = TOKEN_TILE
    row_spec = lambda rows, width: pl.BlockSpec((rows, width), lambda i: (i, 0))
    bf16 = jnp.bfloat16
    views = [(tm, n_tok, ATTN_WIDTH),
             (tm // RES_STEP, n_tok // RES_STEP, RES_STEP * ATTN_WIDTH),
             (tm // RES_STEP ** 2, n_tok // RES_STEP ** 2, RES_STEP ** 2 * ATTN_WIDTH)]
    qkv_specs = [row_spec(r, w) for r, _, w in views for _ in range(3)]
    qkv_shapes = [jax.ShapeDtypeStruct((n, w), bf16) for _, n, w in views for _ in range(3)]
    return pl.pallas_call(
        _front_kernel,
        grid=(n_tok // tm,),
        in_specs=[row_spec(tm, D_MODEL),
                  pl.BlockSpec((None, 3 * N_SUBLAYERS, D_MODEL),
                               lambda i: (batch_offset + i // tiles_per_batch, 0, 0)),
                  _const_spec((1, D_MODEL)),
                  _const_spec((D_MODEL, D_FF)), _const_spec((D_MODEL, D_FF)),
                  _const_spec((D_FF, D_MODEL)),
                  _const_spec((1, D_MODEL)),
                  _const_spec((D_MODEL, IN_WIDTH)),
                  _const_spec((1, SGU_WIDTH)),
                  _const_spec((SGU_GROUPS, SGU_CHUNK, SGU_CHUNK)),
                  _const_spec((SGU_CHUNK, SGU_WIDTH))],
        out_specs=[row_spec(tm, D_MODEL), row_spec(tm, SGU_WIDTH)] + qkv_specs,
        out_shape=[jax.ShapeDtypeStruct((n_tok, D_MODEL), jnp.float32),
                   jax.ShapeDtypeStruct((n_tok, SGU_WIDTH), bf16)] + qkv_shapes,
        scratch_shapes=[pltpu.VMEM((ROW_PARTS, QKV_SLABS, tm // ROW_PARTS, V7X_LANES), jnp.float32),
                        pltpu.VMEM((ROW_PARTS, RES_STEP * QKV_SLABS, tm // ROW_PARTS // RES_STEP, V7X_LANES),
                                   jnp.float32)],
        compiler_params=pltpu.CompilerParams(
            dimension_semantics=("arbitrary",), vmem_limit_bytes=VMEM_LIMIT),
        name="front_ffn1_inproj_sgu",
    )(x2d, mod3, n1, wg, wu, wd, nm, win, sn, sw, sb_full)


def _attn_kernel(q_ref, kp_ref, km_ref, kn_ref, vp_ref, vm_ref, vn_ref, o_ref, lse_ref,
                 bias_ref, *scratch, dil, tq, seq_len, n_res):
    interleave = n_res > 1
    n_blocks = tq // Q_BLOCK
    if interleave:
        o_s, lse_s = scratch

    i = pl.program_id(2)
    first_step = (pl.program_id(0) == 0) & (pl.program_id(1) == 0) & (i == 0)
    pair_w = 2 * Q_BLOCK

    @pl.when(first_step)
    def _():
        key = lax.broadcasted_iota(jnp.int32, (K_BLOCK, pair_w), 0)
        col = lax.broadcasted_iota(jnp.int32, (K_BLOCK, pair_w), 1)
        query = jnp.where(col < Q_BLOCK, col, col - Q_BLOCK)
        rel = jnp.abs(key - HALO - query)
        dist = (dil * rel).astype(jnp.float32)
        before_start = key < HALO
        past_end = key >= HALO + Q_BLOCK
        for pair in range(N_PAIRS):
            slope_a = 2.0 ** (-8.0 * (2 * pair + 1) / ATTN_HEADS)
            slope_b = 2.0 ** (-8.0 * (2 * pair + 2) / ATTN_HEADS)
            slope = jnp.where(col < Q_BLOCK, slope_a * LOG2E, slope_b * LOG2E)
            plain = jnp.where(rel <= HALO, -slope * dist, NEG)
            bias_ref[0, pair] = plain
            bias_ref[1, pair] = jnp.where(before_start, NEG, plain)
            bias_ref[2, pair] = jnp.where(past_end, NEG, plain)
            bias_ref[3, pair] = jnp.where(before_start | past_end, NEG, plain)

    lane = lax.broadcasted_iota(jnp.int32, (1, HEAD_PAIR), 1)
    first_head = lane < HEAD_DIM
    last_tile = seq_len // tq - 1

    def bias_variant(j):
        variant = 0
        if j == 0:
            variant = variant + jnp.where(i == 0, 1, 0)
        if j == n_blocks - 1:
            variant = variant + jnp.where(i == last_tile, 2, 0)
        return variant

    def block_operands(res, j, pair):
        qrows = slice(j * Q_BLOCK, (j + 1) * Q_BLOCK)
        cols = slice(res * ATTN_WIDTH + pair * HEAD_PAIR,
                     res * ATTN_WIDTH + (pair + 1) * HEAD_PAIR)
        return qrows, cols

    def window(prev, main, nxt, j, cols):
        lo, hi = j * Q_BLOCK - HALO, (j + 1) * Q_BLOCK + HALO
        parts = []
        if lo < 0:
            parts.append(prev[:, cols])
        parts.append(main[max(lo, 0):min(hi, tq), cols])
        if hi > tq:
            parts.append(nxt[:, cols])
        return parts[0] if len(parts) == 1 else jnp.concatenate(parts, axis=0)

    def scores(res, j, pair):
        qrows, cols = block_operands(res, j, pair)
        qp = q_ref[qrows, cols]
        zero = jnp.zeros_like(qp)
        qstack = jnp.concatenate([jnp.where(first_head, qp, zero), jnp.where(first_head, zero, qp)], axis=0)
        kp = window(kp_ref, km_ref, kn_ref, j, cols)
        return lax.dot_general(kp, qstack, (((1,), (1,)), ((), ())),
                               preferred_element_type=jnp.float32)

    def softmax_pv(res, j, pair, s):
        qrows, cols = block_operands(res, j, pair)
        s = s + bias_ref[bias_variant(j), pair]
        m = jnp.max(s, axis=0, keepdims=True)
        p = jnp.exp2(s - m).astype(jnp.bfloat16)
        vp = window(vp_ref, vm_ref, vn_ref, j, cols)
        one = jnp.ones_like(vp)
        out_a = lax.dot_general(jnp.where(first_head, vp, one), p[:, :Q_BLOCK],
                                (((0,), (0,)), ((), ())), preferred_element_type=jnp.float32)
        out_b = lax.dot_general(jnp.where(first_head, one, vp), p[:, Q_BLOCK:],
                                (((0,), (0,)), ((), ())), preferred_element_type=jnp.float32)
        sum_a = out_a[HEAD_DIM:HEAD_DIM + V7X_SUBLANES]
        sum_b = out_b[:V7X_SUBLANES]
        reps = HEAD_DIM // V7X_SUBLANES
        rows = lambda t: jnp.concatenate([t] * reps, axis=0)
        o_t = jnp.concatenate([out_a[:HEAD_DIM] * rows(1.0 / sum_a),
                               out_b[HEAD_DIM:] * rows(1.0 / sum_b)], axis=0)
        lse_t = jnp.concatenate([rows(m[:, :Q_BLOCK] + jnp.log2(sum_a)),
                                 rows(m[:, Q_BLOCK:] + jnp.log2(sum_b))], axis=0)
        o, lse = o_t.T, lse_t.T
        if interleave:
            dst = pl.ds(n_res * j * Q_BLOCK + res, Q_BLOCK, stride=n_res)
            o_s[pair, dst, :] = o
            lse_s[pair, dst, :] = lse
        else:
            o_ref[qrows, cols] = o.astype(o_ref.dtype)
            lse_ref[qrows, cols] = lse

    for res in range(n_res):
        for j in range(n_blocks):
            for pair in range(N_PAIRS):
                softmax_pv(res, j, pair, scores(res, j, pair))

    if interleave:
        for pair in range(N_PAIRS):
            cols = slice(pair * HEAD_PAIR, (pair + 1) * HEAD_PAIR)
            o_ref[:, cols] = o_s[pair].astype(o_ref.dtype)
            lse_ref[:, cols] = lse_s[pair]


def _attention(q, k, v, *, dil, n_res, tq, o_dtype):
    b, seq_len, width = q.shape
    n_groups = width // (n_res * ATTN_WIDTH)
    blk_w = n_res * ATTN_WIDTH
    halo_blocks = tq // HALO
    last_halo = seq_len // HALO - 1
    main = pl.BlockSpec((None, tq, blk_w), lambda bi, g, i: (bi, i, g))
    prv = pl.BlockSpec((None, HALO, blk_w),
                       lambda bi, g, i: (bi, jnp.maximum(i * halo_blocks - 1, 0), g))
    nxt = pl.BlockSpec((None, HALO, blk_w),
                       lambda bi, g, i: (bi, jnp.minimum((i + 1) * halo_blocks, last_halo), g))
    out_rows = n_res * tq
    out_spec = pl.BlockSpec((None, out_rows, ATTN_WIDTH), lambda bi, g, i: (bi, i, g))
    out_dims = (b, seq_len * n_res, n_groups * ATTN_WIDTH)

    scratch = [pltpu.VMEM((4, N_PAIRS, K_BLOCK, 2 * Q_BLOCK), jnp.float32)]
    if n_res > 1:
        scratch += [pltpu.VMEM((N_PAIRS, out_rows, HEAD_PAIR), jnp.float32)] * 2
    return pl.pallas_call(
        functools.partial(_attn_kernel, dil=dil, tq=tq, seq_len=seq_len, n_res=n_res),
        grid=(b, n_groups, seq_len // tq),
        in_specs=[main, prv, main, nxt, prv, main, nxt],
        out_specs=[out_spec, out_spec],
        out_shape=[jax.ShapeDtypeStruct(out_dims, o_dtype),
                   jax.ShapeDtypeStruct(out_dims, jnp.float32)],
        scratch_shapes=scratch,
        compiler_params=pltpu.CompilerParams(
            dimension_semantics=("arbitrary", "arbitrary", "arbitrary"),
            vmem_limit_bytes=VMEM_LIMIT),
        name=f"band_attn_d{dil}",
    )(q, k, k, k, v, v, v)


def _back_kernel(x1_ref, mod_ref, gated_ref, o16_ref, l16_ref, o4_ref, l4_ref, o1_ref, l1_ref,
                 wout_ref, n2_ref, wg_ref, wu_ref, wd_ref, nf_ref, y_ref, o16n_ref, l16n_ref):
    gate1 = mod_ref[5:6, :]
    shift2, scale2, gate2 = mod_ref[6:7, :], mod_ref[7:8, :], mod_ref[8:9, :]

    slabs = ATTN_WIDTH // V7X_LANES
    rows4 = TOKEN_TILE // RES_STEP
    for c in range(RES_STEP):
        for sl in range(slabs):
            lanes = slice(c * ATTN_WIDTH + sl * V7X_LANES, c * ATTN_WIDTH + (sl + 1) * V7X_LANES)
            dst = pl.ds(c, rows4, stride=RES_STEP)
            o16n_ref[sl, dst, :] = o16_ref[:, lanes]
            l16n_ref[sl, dst, :] = l16_ref[:, lanes]

    def merged(rows):
        parts = []
        for sl in range(slabs):
            lanes = slice(sl * V7X_LANES, (sl + 1) * V7X_LANES)
            l16, l4, l1 = l16n_ref[sl, rows, :], l4_ref[rows, lanes], l1_ref[rows, lanes]
            top = jnp.maximum(jnp.maximum(l16, l4), l1)
            w16, w4, w1 = jnp.exp2(l16 - top), jnp.exp2(l4 - top), jnp.exp2(l1 - top)
            num = (w16 * o16n_ref[sl, rows, :] + w4 * o4_ref[rows, lanes].astype(jnp.float32)
                   + w1 * o1_ref[rows, lanes].astype(jnp.float32))
            parts.append((num * (1.0 / (w16 + w4 + w1))).astype(jnp.bfloat16))
        return jnp.concatenate(parts, axis=1)

    def mixed(rows, attn):
        mix = (_dot(attn, wout_ref[0:ATTN_WIDTH, :])
               + _dot(gated_ref[rows, :], wout_ref[ATTN_WIDTH:, :]))
        x2 = x1_ref[rows, :] + gate1 * mix
        h = (_rms(x2) * n2_ref[...]) * (1.0 + scale2) + shift2
        return x2, h.astype(jnp.bfloat16)

    parts = [slice(k * TOKEN_TILE // ROW_PARTS, (k + 1) * TOKEN_TILE // ROW_PARTS) for k in range(ROW_PARTS)]
    attn = [merged(rows) for rows in parts]
    staged = [mixed(rows, a) for rows, a in zip(parts, attn)]
    ffn = [_swiglu(h, wg_ref, wu_ref, wd_ref) for _, h in staged]
    for rows, (x2, _), f in zip(parts, staged, ffn):
        x3 = x2 + (FFN_RESID * gate2) * f
        y_ref[rows, :] = _rms(x3) * nf_ref[...]


def _back(x1, mod3, batch_offset, tiles_per_batch, gated, res16, res4, res1,
          wout, n2, wg, wu, wd, nf):
    n_tok = x1.shape[0]
    tm = TOKEN_TILE
    row_spec = lambda width: pl.BlockSpec((tm, width), lambda i: (i, 0))
    view4_spec = pl.BlockSpec((tm // RES_STEP, RES_STEP * ATTN_WIDTH), lambda i: (i, 0))
    slabs = ATTN_WIDTH // V7X_LANES
    return pl.pallas_call(
        _back_kernel,
        grid=(n_tok // tm,),
        in_specs=[row_spec(D_MODEL),
                  pl.BlockSpec((None, 3 * N_SUBLAYERS, D_MODEL),
                               lambda i: (batch_offset + i // tiles_per_batch, 0, 0)),
                  row_spec(SGU_WIDTH),
                  view4_spec, view4_spec,
                  row_spec(ATTN_WIDTH), row_spec(ATTN_WIDTH),
                  row_spec(ATTN_WIDTH), row_spec(ATTN_WIDTH),
                  _const_spec((D_MODEL, D_MODEL)),
                  _const_spec((1, D_MODEL)),
                  _const_spec((D_MODEL, D_FF)), _const_spec((D_MODEL, D_FF)),
                  _const_spec((D_FF, D_MODEL)),
                  _const_spec((1, D_MODEL))],
        out_specs=row_spec(D_MODEL),
        out_shape=jax.ShapeDtypeStruct((n_tok, D_MODEL), jnp.float32),
        scratch_shapes=[pltpu.VMEM((slabs, tm, V7X_LANES), jnp.float32)] * 2,
        compiler_params=pltpu.CompilerParams(
            dimension_semantics=("arbitrary",), vmem_limit_bytes=VMEM_LIMIT),
        name="back_merge_outproj_ffn2",
    )(x1, mod3, gated, *res16, *res4, *res1, wout, n2, wg, wu, wd, nf)


def kernel(x_prompt, x_sample, c_prompt, c_sample, ada_w, ada_b, ffn1_norm, ffn1_w_gate, ffn1_w_up, ffn1_w_down, mix_norm, w_in, sgu_norm, sgu_w, sgu_b, w_out, ffn2_norm, ffn2_w_gate, ffn2_w_up, ffn2_w_down, final_norm):
    bf = lambda w: w.astype(jnp.bfloat16)
    row = lambda g: g.reshape(1, -1)
    assert ada_w.shape[0] == 1, "single-layer stack"

    c_all = jnp.concatenate([c_prompt, c_sample], axis=0)
    mod3 = _modulation(c_all, ada_w[0], ada_b[0]).reshape(-1, 3 * N_SUBLAYERS, D_MODEL)

    wg1, wu1, wd1 = bf(ffn1_w_gate[0]), bf(ffn1_w_up[0]), bf(ffn1_w_down[0])
    wg2, wu2, wd2 = bf(ffn2_w_gate[0]), bf(ffn2_w_up[0]), bf(ffn2_w_down[0])
    win, wout, sw = bf(w_in[0]), bf(w_out[0]), bf(sgu_w[0])
    sb_full = jnp.repeat(sgu_b[0].T, SGU_GROUP_DIM, axis=1)

    def encode(x, batch_offset):
        b, s, _ = x.shape
        tiles_per_batch = s // TOKEN_TILE
        x2d = x.reshape(b * s, D_MODEL)
        x1, gated, q1, k1, v1, q4, k4, v4, q16, k16, v16 = _front(
            x2d, mod3, batch_offset, tiles_per_batch, row(ffn1_norm[0]), wg1, wu1, wd1,
            row(mix_norm[0]), win, row(sgu_norm[0]), sw, sb_full)
        per_batch = lambda t: t.reshape(b, t.shape[0] // b, t.shape[1])
        res16 = _attention(per_batch(q16), per_batch(k16), per_batch(v16),
                           dil=16, n_res=RES_STEP, tq=2 * Q_BLOCK, o_dtype=jnp.float32)
        res4 = _attention(per_batch(q4), per_batch(k4), per_batch(v4),
                          dil=4, n_res=RES_STEP, tq=2 * Q_BLOCK, o_dtype=jnp.bfloat16)
        res1 = _attention(per_batch(q1), per_batch(k1), per_batch(v1),
                          dil=1, n_res=1, tq=8 * Q_BLOCK, o_dtype=jnp.bfloat16)
        flat = lambda t: t.reshape(t.shape[0] * t.shape[1], t.shape[2])
        y = _back(x1, mod3, batch_offset, tiles_per_batch, gated,
                  [flat(t) for t in res16], [flat(t) for t in res4], [flat(t) for t in res1],
                  wout, row(ffn2_norm[0]), wg2, wu2, wd2, row(final_norm))
        return y.reshape(b, s, D_MODEL)

    y_prompt = encode(x_prompt, 0)
    y_sample = encode(x_sample, x_prompt.shape[0])
    return (y_prompt, y_sample)
```

```python
import functools
import math

import jax
import jax.numpy as jnp
from jax import lax
from jax.experimental import pallas as pl
from jax.experimental.pallas import tpu as pltpu

D_MODEL = 1024
ATTN_HEADS = 8
HEAD_DIM = 64
ATTN_WIDTH = ATTN_HEADS * HEAD_DIM
DILATED_PATTERNS = ((128, 1), (512, 4), (2048, 16))
SGU_WIDTH = D_MODEL - ATTN_WIDTH
SGU_GROUPS = 4
SGU_GROUP_DIM = SGU_WIDTH // SGU_GROUPS
SGU_CHUNK = 128
IN_WIDTH = 3 * ATTN_WIDTH + 2 * SGU_WIDTH
D_FF = 2816
FFN_RESID = 0.5
N_SUBLAYERS = 3
EPS = 1e-6
NEG = -1e30
LOG2E = math.log2(math.e)

V7X_LANES = 128
V7X_SUBLANES = 8
V7X_VMEM_BYTES = 64 * 1024 * 1024
VMEM_LIMIT = V7X_VMEM_BYTES - 8 * 1024 * 1024

TOKEN_TILE = 512
ROW_PARTS = 2
FF_CHUNKS = ((0, 1024), (1024, 1024), (2048, 768))
Q_BLOCK = 128
HALO = 64
K_BLOCK = Q_BLOCK + 2 * HALO
HEAD_PAIR = 2 * HEAD_DIM
N_PAIRS = ATTN_HEADS // 2
QKV_SLABS = 3 * ATTN_WIDTH // V7X_LANES
RES_STEP = 4

assert all(w // (2 * d) == HALO for w, d in DILATED_PATTERNS)
assert [d for _, d in DILATED_PATTERNS] == [1, RES_STEP, RES_STEP * RES_STEP]
assert HEAD_PAIR == V7X_LANES


def _const_spec(shape):
    zeros = (0,) * len(shape)
    return pl.BlockSpec(shape, lambda *_: zeros, pipeline_mode=pl.Buffered(1))


def _rms(x):
    return x * lax.rsqrt(jnp.mean(x * x, axis=-1, keepdims=True) + EPS)


def _silu(x):
    return x * (1.0 / (1.0 + jnp.exp(-x)))


def _gelu_tanh(x):
    c = math.sqrt(2.0 / math.pi)
    return 0.5 * x * (1.0 + jnp.tanh(c * (x + 0.044715 * (x * x * x))))


def _dot(a, b):
    return jnp.dot(a, b, preferred_element_type=jnp.float32)


def _swiglu(h, wg_ref, wu_ref, wd_ref):
    acc = None
    for start, size in FF_CHUNKS:
        g = _dot(h, wg_ref[:, start:start + size])
        u = _dot(h, wu_ref[:, start:start + size])
        a = (_silu(g) * u).astype(jnp.bfloat16)
        part = _dot(a, wd_ref[start:start + size, :])
        acc = part if acc is None else acc + part
    return acc


def _mod_kernel(c_ref, w_ref, b_ref, o_ref):
    c = c_ref[...]
    o_ref[...] = jnp.dot(_silu(c), w_ref[...], preferred_element_type=jnp.float32,
                         precision=lax.Precision.HIGHEST) + b_ref[...]


def _modulation(c_all, ada_w, ada_b):
    nb = c_all.shape[0]
    width = ada_w.shape[1]
    col = 1536
    return pl.pallas_call(
        _mod_kernel,
        grid=(width // col,),
        in_specs=[pl.BlockSpec((nb, D_MODEL), lambda j: (0, 0)),
                  pl.BlockSpec((D_MODEL, col), lambda j: (0, j)),
                  pl.BlockSpec((1, col), lambda j: (0, j))],
        out_specs=pl.BlockSpec((nb, col), lambda j: (0, j)),
        out_shape=jax.ShapeDtypeStruct((nb, width), jnp.float32),
        name="adaln_mod",
    )(c_all, ada_w, ada_b.reshape(1, width))


def _front_kernel(x_ref, mod_ref, n1_ref, wg_ref, wu_ref, wd_ref, nm_ref, win_ref,
                  sn_ref, sw_ref, sb_ref,
                  x1_ref, gated_ref, q1_ref, k1_ref, v1_ref, q4_ref, k4_ref, v4_ref,
                  q16_ref, k16_ref, v16_ref, zs_ref, zs4_ref):
    shift0, scale0, gate0 = mod_ref[0:1, :], mod_ref[1:2, :], mod_ref[2:3, :]
    shift1, scale1 = mod_ref[3:4, :], mod_ref[4:5, :]
    slabs_per = ATTN_WIDTH // V7X_LANES
    outs1 = (q1_ref, k1_ref, v1_ref)
    outs4 = (q4_ref, k4_ref, v4_ref)
    outs16 = (q16_ref, k16_ref, v16_ref)
    part_rows = TOKEN_TILE // ROW_PARTS
    rows4 = part_rows // RES_STEP
    rows16 = rows4 // RES_STEP

    def ffn_input(r0):
        x = x_ref[r0:r0 + part_rows, :]
        return ((_rms(x) * n1_ref[...]) * (1.0 + scale0) + shift0).astype(jnp.bfloat16)

    def mixer_input(r0, ffn):
        x1 = x_ref[r0:r0 + part_rows, :] + (FFN_RESID * gate0) * ffn
        x1_ref[r0:r0 + part_rows, :] = x1
        return ((_rms(x1) * nm_ref[...]) * (1.0 + scale1) + shift1).astype(jnp.bfloat16)

    def gating(r0, z):
        u = _gelu_tanh(z[:, 3 * ATTN_WIDTH:3 * ATTN_WIDTH + SGU_WIDTH])
        sv = _gelu_tanh(z[:, 3 * ATTN_WIDTH + SGU_WIDTH:])
        sv = (_rms(sv) * sn_ref[...]).astype(jnp.bfloat16)
        for n in range(part_rows // SGU_CHUNK):
            rows = slice(n * SGU_CHUNK, (n + 1) * SGU_CHUNK)
            mixed = jnp.concatenate(
                [_dot(sw_ref[g], sv[rows, g * SGU_GROUP_DIM:(g + 1) * SGU_GROUP_DIM])
                 for g in range(SGU_GROUPS)], axis=1) + sb_ref[...]
            gated_ref[r0 + n * SGU_CHUNK:r0 + (n + 1) * SGU_CHUNK, :] = (
                u[rows, :] * mixed).astype(jnp.bfloat16)

    def relayout(part, z):
        r0 = part * part_rows
        for s in range(QKV_SLABS):
            zslab = z[:, s * V7X_LANES:(s + 1) * V7X_LANES]
            if s < slabs_per:
                zslab = zslab * (HEAD_DIM ** -0.5 * LOG2E)
            zs_ref[part, s] = zslab
        for s in range(QKV_SLABS):
            t, ss = divmod(s, slabs_per)
            lanes = slice(ss * V7X_LANES, (ss + 1) * V7X_LANES)
            outs1[t][r0:r0 + part_rows, lanes] = zs_ref[part, s].astype(jnp.bfloat16)
            for c in range(RES_STEP):
                piece = zs_ref[part, s, pl.ds(c, rows4, stride=RES_STEP), :]
                zs4_ref[part, c * QKV_SLABS + s] = piece
                base = c * ATTN_WIDTH + ss * V7X_LANES
                outs4[t][part * rows4:(part + 1) * rows4, base:base + V7X_LANES] = piece.astype(jnp.bfloat16)
        for s in range(QKV_SLABS):
            t, ss = divmod(s, slabs_per)
            for c in range(RES_STEP):
                for a in range(RES_STEP):
                    piece = zs4_ref[part, c * QKV_SLABS + s, pl.ds(a, rows16, stride=RES_STEP), :]
                    base = (c * RES_STEP + a) * ATTN_WIDTH + ss * V7X_LANES
                    outs16[t][part * rows16:(part + 1) * rows16, base:base + V7X_LANES] = (
                        piece.astype(jnp.bfloat16))

    starts = [k * part_rows for k in range(ROW_PARTS)]
    h0 = [ffn_input(r0) for r0 in starts]
    ffn = [_swiglu(h, wg_ref, wu_ref, wd_ref) for h in h0]
    h1 = [mixer_input(r0, f) for r0, f in zip(starts, ffn)]
    z = [_dot(h, win_ref[...]) for h in h1]
    for r0, zp in zip(starts, z):
        gating(r0, zp)
    for part, zp in enumerate(z):
        relayout(part, zp)


def _front(x2d, mod3, batch_offset, tiles_per_batch, n1, wg, wu, wd, nm, win, sn, sw, sb_full):
    n_tok = x2d.shape[0]
    tm = TOKEN_TILE
    row_spec = lambda rows, width: pl.BlockSpec((rows, width), lambda i: (i, 0))
    bf16 = jnp.bfloat16
    views = [(tm, n_tok, ATTN_WIDTH),
             (tm // RES_STEP, n_tok // RES_STEP, RES_STEP * ATTN_WIDTH),
             (tm // RES_STEP ** 2, n_tok // RES_STEP ** 2, RES_STEP ** 2 * ATTN_WIDTH)]
    qkv_specs = [row_spec(r, w) for r, _, w in views for _ in range(3)]
    qkv_shapes = [jax.ShapeDtypeStruct((n, w), bf16) for _, n, w in views for _ in range(3)]
    return pl.pallas_call(
        _front_kernel,
        grid=(n_tok // tm,),
        in_specs=[row_spec(tm, D_MODEL),
                  pl.BlockSpec((None, 3 * N_SUBLAYERS, D_MODEL),
                               lambda i: (batch_offset + i // tiles_per_batch, 0, 0)),
                  _const_spec((1, D_MODEL)),
                  _const_spec((D_MODEL, D_FF)), _const_spec((D_MODEL, D_FF)),
                  _const_spec((D_FF, D_MODEL)),
                  _const_spec((1, D_MODEL)),
                  _const_spec((D_MODEL, IN_WIDTH)),
                  _const_spec((1, SGU_WIDTH)),
                  _const_spec((SGU_GROUPS, SGU_CHUNK, SGU_CHUNK)),
                  _const_spec((SGU_CHUNK, SGU_WIDTH))],
        out_specs=[row_spec(tm, D_MODEL), row_spec(tm, SGU_WIDTH)] + qkv_specs,
        out_shape=[jax.ShapeDtypeStruct((n_tok, D_MODEL), jnp.float32),
                   jax.ShapeDtypeStruct((n_tok, SGU_WIDTH), bf16)] + qkv_shapes,
        scratch_shapes=[pltpu.VMEM((ROW_PARTS, QKV_SLABS, tm // ROW_PARTS, V7X_LANES), jnp.float32),
                        pltpu.VMEM((ROW_PARTS, RES_STEP * QKV_SLABS, tm // ROW_PARTS // RES_STEP, V7X_LANES),
                                   jnp.float32)],
        compiler_params=pltpu.CompilerParams(
            dimension_semantics=("arbitrary",), vmem_limit_bytes=VMEM_LIMIT),
        name="front_ffn1_inproj_sgu",
    )(x2d, mod3, n1, wg, wu, wd, nm, win, sn, sw, sb_full)


def _attn_kernel(q_ref, kp_ref, km_ref, kn_ref, vp_ref, vm_ref, vn_ref, o_ref, lse_ref,
                 bias_ref, *scratch, dil, tq, seq_len, n_res):
    interleave = n_res > 1
    n_blocks = tq // Q_BLOCK
    if interleave:
        o_s, lse_s = scratch

    i = pl.program_id(2)
    first_step = (pl.program_id(0) == 0) & (pl.program_id(1) == 0) & (i == 0)
    pair_w = 2 * Q_BLOCK

    @pl.when(first_step)
    def _():
        key = lax.broadcasted_iota(jnp.int32, (K_BLOCK, pair_w), 0)
        col = lax.broadcasted_iota(jnp.int32, (K_BLOCK, pair_w), 1)
        query = jnp.where(col < Q_BLOCK, col, col - Q_BLOCK)
        rel = jnp.abs(key - HALO - query)
        dist = (dil * rel).astype(jnp.float32)
        before_start = key < HALO
        past_end = key >= HALO + Q_BLOCK
        for pair in range(N_PAIRS):
            slope_a = 2.0 ** (-8.0 * (2 * pair + 1) / ATTN_HEADS)
            slope_b = 2.0 ** (-8.0 * (2 * pair + 2) / ATTN_HEADS)
            slope = jnp.where(col < Q_BLOCK, slope_a * LOG2E, slope_b * LOG2E)
            plain = jnp.where(rel <= HALO, -slope * dist, NEG)
            bias_ref[0, pair] = plain
            bias_ref[1, pair] = jnp.where(before_start, NEG, plain)
            bias_ref[2, pair] = jnp.where(past_end, NEG, plain)
            bias_ref[3, pair] = jnp.where(before_start | past_end, NEG, plain)

    lane = lax.broadcasted_iota(jnp.int32, (1, HEAD_PAIR), 1)
    first_head = lane < HEAD_DIM
    last_tile = seq_len // tq - 1

    def bias_variant(j):
        variant = 0
        if j == 0:
            variant = variant + jnp.where(i == 0, 1, 0)
        if j == n_blocks - 1:
            variant = variant + jnp.where(i == last_tile, 2, 0)
        return variant

    def block_operands(res, j, pair):
        qrows = slice(j * Q_BLOCK, (j + 1) * Q_BLOCK)
        cols = slice(res * ATTN_WIDTH + pair * HEAD_PAIR,
                     res * ATTN_WIDTH + (pair + 1) * HEAD_PAIR)
        return qrows, cols

    def window(prev, main, nxt, j, cols):
        lo, hi = j * Q_BLOCK - HALO, (j + 1) * Q_BLOCK + HALO
        parts = []
        if lo < 0:
            parts.append(prev[:, cols])
        parts.append(main[max(lo, 0):min(hi, tq), cols])
        if hi > tq:
            parts.append(nxt[:, cols])
        return parts[0] if len(parts) == 1 else jnp.concatenate(parts, axis=0)

    values_cache = {}

    def values_t(res, pair):
        if (res, pair) not in values_cache:
            cols = block_operands(res, 0, pair)[1]
            rows = jnp.concatenate([vp_ref[:, cols], vm_ref[:, cols], vn_ref[:, cols]], axis=0)
            values_cache[(res, pair)] = rows.T
        return values_cache[(res, pair)]

    def scores(res, j, pair):
        qrows, cols = block_operands(res, j, pair)
        qp = q_ref[qrows, cols]
        zero = jnp.zeros_like(qp)
        qstack = jnp.concatenate([jnp.where(first_head, qp, zero), jnp.where(first_head, zero, qp)], axis=0)
        kp = window(kp_ref, km_ref, kn_ref, j, cols)
        return lax.dot_general(kp, qstack, (((1,), (1,)), ((), ())),
                               preferred_element_type=jnp.float32)

    def softmax_pv(res, j, pair, s):
        qrows, cols = block_operands(res, j, pair)
        s = s + bias_ref[bias_variant(j), pair]
        m = jnp.max(s, axis=0, keepdims=True)
        p = jnp.exp2(s - m).astype(jnp.bfloat16)
        v_t = values_t(res, pair)[:, j * Q_BLOCK:j * Q_BLOCK + K_BLOCK]
        one = jnp.ones((HEAD_DIM, K_BLOCK), v_t.dtype)
        out_a = _dot(jnp.concatenate([v_t[:HEAD_DIM], one], axis=0), p[:, :Q_BLOCK])
        out_b = _dot(jnp.concatenate([one, v_t[HEAD_DIM:]], axis=0), p[:, Q_BLOCK:])
        sum_a = out_a[HEAD_DIM:HEAD_DIM + V7X_SUBLANES]
        sum_b = out_b[:V7X_SUBLANES]
        reps = HEAD_DIM // V7X_SUBLANES
        rows = lambda t: jnp.concatenate([t] * reps, axis=0)
        o_t = jnp.concatenate([out_a[:HEAD_DIM] * rows(1.0 / sum_a),
                               out_b[HEAD_DIM:] * rows(1.0 / sum_b)], axis=0)
        lse_t = jnp.concatenate([rows(m[:, :Q_BLOCK] + jnp.log2(sum_a)),
                                 rows(m[:, Q_BLOCK:] + jnp.log2(sum_b))], axis=0)
        o, lse = o_t.T, lse_t.T
        if interleave:
            dst = pl.ds(n_res * j * Q_BLOCK + res, Q_BLOCK, stride=n_res)
            o_s[pair, dst, :] = o
            lse_s[pair, dst, :] = lse
        else:
            o_ref[qrows, cols] = o.astype(o_ref.dtype)
            lse_ref[qrows, cols] = lse

    for res in range(n_res):
        for j in range(n_blocks):
            for pair in range(N_PAIRS):
                softmax_pv(res, j, pair, scores(res, j, pair))

    if interleave:
        for pair in range(N_PAIRS):
            cols = slice(pair * HEAD_PAIR, (pair + 1) * HEAD_PAIR)
            o_ref[:, cols] = o_s[pair].astype(o_ref.dtype)
            lse_ref[:, cols] = lse_s[pair]


def _attention(q, k, v, *, dil, n_res, tq, o_dtype):
    b, seq_len, width = q.shape
    n_groups = width // (n_res * ATTN_WIDTH)
    blk_w = n_res * ATTN_WIDTH
    halo_blocks = tq // HALO
    last_halo = seq_len // HALO - 1
    main = pl.BlockSpec((None, tq, blk_w), lambda bi, g, i: (bi, i, g))
    prv = pl.BlockSpec((None, HALO, blk_w),
                       lambda bi, g, i: (bi, jnp.maximum(i * halo_blocks - 1, 0), g))
    nxt = pl.BlockSpec((None, HALO, blk_w),
                       lambda bi, g, i: (bi, jnp.minimum((i + 1) * halo_blocks, last_halo), g))
    out_rows = n_res * tq
    out_spec = pl.BlockSpec((None, out_rows, ATTN_WIDTH), lambda bi, g, i: (bi, i, g))
    out_dims = (b, seq_len * n_res, n_groups * ATTN_WIDTH)

    scratch = [pltpu.VMEM((4, N_PAIRS, K_BLOCK, 2 * Q_BLOCK), jnp.float32)]
    if n_res > 1:
        scratch += [pltpu.VMEM((N_PAIRS, out_rows, HEAD_PAIR), jnp.float32)] * 2
    return pl.pallas_call(
        functools.partial(_attn_kernel, dil=dil, tq=tq, seq_len=seq_len, n_res=n_res),
        grid=(b, n_groups, seq_len // tq),
        in_specs=[main, prv, main, nxt, prv, main, nxt],
        out_specs=[out_spec, out_spec],
        out_shape=[jax.ShapeDtypeStruct(out_dims, o_dtype),
                   jax.ShapeDtypeStruct(out_dims, jnp.float32)],
        scratch_shapes=scratch,
        compiler_params=pltpu.CompilerParams(
            dimension_semantics=("arbitrary", "arbitrary", "arbitrary"),
            vmem_limit_bytes=VMEM_LIMIT),
        name=f"band_attn_d{dil}",
    )(q, k, k, k, v, v, v)


def _back_kernel(x1_ref, mod_ref, gated_ref, o16_ref, l16_ref, o4_ref, l4_ref, o1_ref, l1_ref,
                 wout_ref, n2_ref, wg_ref, wu_ref, wd_ref, nf_ref, y_ref, o16n_ref, l16n_ref):
    gate1 = mod_ref[5:6, :]
    shift2, scale2, gate2 = mod_ref[6:7, :], mod_ref[7:8, :], mod_ref[8:9, :]

    slabs = ATTN_WIDTH // V7X_LANES
    rows4 = TOKEN_TILE // RES_STEP
    for c in range(RES_STEP):
        for sl in range(slabs):
            lanes = slice(c * ATTN_WIDTH + sl * V7X_LANES, c * ATTN_WIDTH + (sl + 1) * V7X_LANES)
            dst = pl.ds(c, rows4, stride=RES_STEP)
            o16n_ref[sl, dst, :] = o16_ref[:, lanes]
            l16n_ref[sl, dst, :] = l16_ref[:, lanes]

    def merged(rows):
        parts = []
        for sl in range(slabs):
            lanes = slice(sl * V7X_LANES, (sl + 1) * V7X_LANES)
            l16, l4, l1 = l16n_ref[sl, rows, :], l4_ref[rows, lanes], l1_ref[rows, lanes]
            top = jnp.maximum(jnp.maximum(l16, l4), l1)
            w16, w4, w1 = jnp.exp2(l16 - top), jnp.exp2(l4 - top), jnp.exp2(l1 - top)
            num = (w16 * o16n_ref[sl, rows, :] + w4 * o4_ref[rows, lanes].astype(jnp.float32)
                   + w1 * o1_ref[rows, lanes].astype(jnp.float32))
            parts.append((num * (1.0 / (w16 + w4 + w1))).astype(jnp.bfloat16))
        return jnp.concatenate(parts, axis=1)

    def mixed(rows, attn):
        mix = (_dot(attn, wout_ref[0:ATTN_WIDTH, :])
               + _dot(gated_ref[rows, :], wout_ref[ATTN_WIDTH:, :]))
        x2 = x1_ref[rows, :] + gate1 * mix
        h = (_rms(x2) * n2_ref[...]) * (1.0 + scale2) + shift2
        return x2, h.astype(jnp.bfloat16)

    parts = [slice(k * TOKEN_TILE // ROW_PARTS, (k + 1) * TOKEN_TILE // ROW_PARTS) for k in range(ROW_PARTS)]
    attn = [merged(rows) for rows in parts]
    staged = [mixed(rows, a) for rows, a in zip(parts, attn)]
    ffn = [_swiglu(h, wg_ref, wu_ref, wd_ref) for _, h in staged]
    for rows, (x2, _), f in zip(parts, staged, ffn):
        x3 = x2 + (FFN_RESID * gate2) * f
        y_ref[rows, :] = _rms(x3) * nf_ref[...]


def _back(x1, mod3, batch_offset, tiles_per_batch, gated, res16, res4, res1,
          wout, n2, wg, wu, wd, nf):
    n_tok = x1.shape[0]
    tm = TOKEN_TILE
    row_spec = lambda width: pl.BlockSpec((tm, width), lambda i: (i, 0))
    view4_spec = pl.BlockSpec((tm // RES_STEP, RES_STEP * ATTN_WIDTH), lambda i: (i, 0))
    slabs = ATTN_WIDTH // V7X_LANES
    return pl.pallas_call(
        _back_kernel,
        grid=(n_tok // tm,),
        in_specs=[row_spec(D_MODEL),
                  pl.BlockSpec((None, 3 * N_SUBLAYERS, D_MODEL),
                               lambda i: (batch_offset + i // tiles_per_batch, 0, 0)),
                  row_spec(SGU_WIDTH),
                  view4_spec, view4_spec,
                  row_spec(ATTN_WIDTH), row_spec(ATTN_WIDTH),
                  row_spec(ATTN_WIDTH), row_spec(ATTN_WIDTH),
                  _const_spec((D_MODEL, D_MODEL)),
                  _const_spec((1, D_MODEL)),
                  _const_spec((D_MODEL, D_FF)), _const_spec((D_MODEL, D_FF)),
                  _const_spec((D_FF, D_MODEL)),
                  _const_spec((1, D_MODEL))],
        out_specs=row_spec(D_MODEL),
        out_shape=jax.ShapeDtypeStruct((n_tok, D_MODEL), jnp.float32),
        scratch_shapes=[pltpu.VMEM((slabs, tm, V7X_LANES), jnp.float32)] * 2,
        compiler_params=pltpu.CompilerParams(
            dimension_semantics=("arbitrary",), vmem_limit_bytes=VMEM_LIMIT),
        name="back_merge_outproj_ffn2",
    )(x1, mod3, gated, *res16, *res4, *res1, wout, n2, wg, wu, wd, nf)


def kernel(x_prompt, x_sample, c_prompt, c_sample, ada_w, ada_b, ffn1_norm, ffn1_w_gate, ffn1_w_up, ffn1_w_down, mix_norm, w_in, sgu_norm, sgu_w, sgu_b, w_out, ffn2_norm, ffn2_w_gate, ffn2_w_up, ffn2_w_down, final_norm):
    bf = lambda w: w.astype(jnp.bfloat16)
    row = lambda g: g.reshape(1, -1)
    assert ada_w.shape[0] == 1, "single-layer stack"

    c_all = jnp.concatenate([c_prompt, c_sample], axis=0)
    mod3 = _modulation(c_all, ada_w[0], ada_b[0]).reshape(-1, 3 * N_SUBLAYERS, D_MODEL)

    wg1, wu1, wd1 = bf(ffn1_w_gate[0]), bf(ffn1_w_up[0]), bf(ffn1_w_down[0])
    wg2, wu2, wd2 = bf(ffn2_w_gate[0]), bf(ffn2_w_up[0]), bf(ffn2_w_down[0])
    win, wout, sw = bf(w_in[0]), bf(w_out[0]), bf(sgu_w[0])
    sb_full = jnp.repeat(sgu_b[0].T, SGU_GROUP_DIM, axis=1)

    def encode(x, batch_offset):
        b, s, _ = x.shape
        tiles_per_batch = s // TOKEN_TILE
        x2d = x.reshape(b * s, D_MODEL)
        x1, gated, q1, k1, v1, q4, k4, v4, q16, k16, v16 = _front(
            x2d, mod3, batch_offset, tiles_per_batch, row(ffn1_norm[0]), wg1, wu1, wd1,
            row(mix_norm[0]), win, row(sgu_norm[0]), sw, sb_full)
        per_batch = lambda t: t.reshape(b, t.shape[0] // b, t.shape[1])
        res16 = _attention(per_batch(q16), per_batch(k16), per_batch(v16),
                           dil=16, n_res=RES_STEP, tq=2 * Q_BLOCK, o_dtype=jnp.float32)
        res4 = _attention(per_batch(q4), per_batch(k4), per_batch(v4),
                          dil=4, n_res=RES_STEP, tq=2 * Q_BLOCK, o_dtype=jnp.bfloat16)
        res1 = _attention(per_batch(q1), per_batch(k1), per_batch(v1),
                          dil=1, n_res=1, tq=8 * Q_BLOCK, o_dtype=jnp.bfloat16)
        flat = lambda t: t.reshape(t.shape[0] * t.shape[1], t.shape[2])
        y = _back(x1, mod3, batch_offset, tiles_per_batch, gated,
                  [flat(t) for t in res16], [flat(t) for t in res4], [flat(t) for t in res1],
                  wout, row(ffn2_norm[0]), wg2, wu2, wd2, row(final_norm))
        return y.reshape(b, s, D_MODEL)

    y_prompt = encode(x_prompt, 0)
    y_sample = encode(x_sample, x_prompt.shape[0])
    return (y_prompt, y_sample)
```

```python
import functools
import math

import jax
import jax.numpy as jnp
from jax import lax
from jax.experimental import pallas as pl
from jax.experimental.pallas import tpu as pltpu

D_MODEL = 1024
ATTN_HEADS = 8
HEAD_DIM = 64
ATTN_WIDTH = ATTN_HEADS * HEAD_DIM
DILATED_PATTERNS = ((128, 1), (512, 4), (2048, 16))
SGU_WIDTH = D_MODEL - ATTN_WIDTH
SGU_GROUPS = 4
SGU_GROUP_DIM = SGU_WIDTH // SGU_GROUPS
SGU_CHUNK = 128
IN_WIDTH = 3 * ATTN_WIDTH + 2 * SGU_WIDTH
D_FF = 2816
FFN_RESID = 0.5
N_SUBLAYERS = 3
EPS = 1e-6
NEG = -1e30
LOG2E = math.log2(math.e)

V7X_LANES = 128
V7X_SUBLANES = 8
V7X_VMEM_BYTES = 64 * 1024 * 1024
VMEM_LIMIT = V7X_VMEM_BYTES - 8 * 1024 * 1024

TOKEN_TILE = 512
ROW_PARTS = 2
FF_CHUNKS = ((0, 1024), (1024, 1024), (2048, 768))
Q_BLOCK = 128
HALO = 64
K_BLOCK = Q_BLOCK + 2 * HALO
HEAD_PAIR = 2 * HEAD_DIM
N_PAIRS = ATTN_HEADS // 2
QKV_SLABS = 3 * ATTN_WIDTH // V7X_LANES
RES_STEP = 4

assert all(w // (2 * d) == HALO for w, d in DILATED_PATTERNS)
assert [d for _, d in DILATED_PATTERNS] == [1, RES_STEP, RES_STEP * RES_STEP]
assert HEAD_PAIR == V7X_LANES


def _const_spec(shape):
    zeros = (0,) * len(shape)
    return pl.BlockSpec(shape, lambda *_: zeros, pipeline_mode=pl.Buffered(1))


def _rms(x):
    return x * lax.rsqrt(jnp.mean(x * x, axis=-1, keepdims=True) + EPS)


def _silu(x):
    return x * (1.0 / (1.0 + jnp.exp(-x)))


def _gelu_tanh(x):
    c = math.sqrt(2.0 / math.pi)
    return 0.5 * x * (1.0 + jnp.tanh(c * (x + 0.044715 * (x * x * x))))


def _dot(a, b):
    return jnp.dot(a, b, preferred_element_type=jnp.float32)


def _swiglu(h, wg_ref, wu_ref, wd_ref):
    acc = None
    for start, size in FF_CHUNKS:
        g = _dot(h, wg_ref[:, start:start + size])
        u = _dot(h, wu_ref[:, start:start + size])
        a = (_silu(g) * u).astype(jnp.bfloat16)
        part = _dot(a, wd_ref[start:start + size, :])
        acc = part if acc is None else acc + part
    return acc


def _mod_kernel(c_ref, w_ref, b_ref, o_ref):
    c = c_ref[...]
    o_ref[...] = jnp.dot(_silu(c), w_ref[...], preferred_element_type=jnp.float32,
                         precision=lax.Precision.HIGHEST) + b_ref[...]


def _modulation(c_all, ada_w, ada_b):
    nb = c_all.shape[0]
    width = ada_w.shape[1]
    col = 1536
    return pl.pallas_call(
        _mod_kernel,
        grid=(width // col,),
        in_specs=[pl.BlockSpec((nb, D_MODEL), lambda j: (0, 0)),
                  pl.BlockSpec((D_MODEL, col), lambda j: (0, j)),
                  pl.BlockSpec((1, col), lambda j: (0, j))],
        out_specs=pl.BlockSpec((nb, col), lambda j: (0, j)),
        out_shape=jax.ShapeDtypeStruct((nb, width), jnp.float32),
        name="adaln_mod",
    )(c_all, ada_w, ada_b.reshape(1, width))


def _front_kernel(x_ref, mod_ref, n1_ref, wg_ref, wu_ref, wd_ref, nm_ref, win_ref,
                  sn_ref, sw_ref, sb_ref,
                  x1_ref, gated_ref, qkv1_ref, qkv4_ref, qkv16_ref, zs_ref, zs4_ref):
    shift0, scale0, gate0 = mod_ref[0:1, :], mod_ref[1:2, :], mod_ref[2:3, :]
    shift1, scale1 = mod_ref[3:4, :], mod_ref[4:5, :]
    slabs_per = ATTN_WIDTH // V7X_LANES
    group_w = RES_STEP * ATTN_WIDTH
    part_rows = TOKEN_TILE // ROW_PARTS
    rows4 = part_rows // RES_STEP
    rows16 = rows4 // RES_STEP

    def ffn_input(r0):
        x = x_ref[r0:r0 + part_rows, :]
        return ((_rms(x) * n1_ref[...]) * (1.0 + scale0) + shift0).astype(jnp.bfloat16)

    def mixer_input(r0, ffn):
        x1 = x_ref[r0:r0 + part_rows, :] + (FFN_RESID * gate0) * ffn
        x1_ref[r0:r0 + part_rows, :] = x1
        return ((_rms(x1) * nm_ref[...]) * (1.0 + scale1) + shift1).astype(jnp.bfloat16)

    def gating(r0, z):
        u = _gelu_tanh(z[:, 3 * ATTN_WIDTH:3 * ATTN_WIDTH + SGU_WIDTH])
        sv = _gelu_tanh(z[:, 3 * ATTN_WIDTH + SGU_WIDTH:])
        sv = (_rms(sv) * sn_ref[...]).astype(jnp.bfloat16)
        for n in range(part_rows // SGU_CHUNK):
            rows = slice(n * SGU_CHUNK, (n + 1) * SGU_CHUNK)
            mixed = jnp.concatenate(
                [_dot(sw_ref[g], sv[rows, g * SGU_GROUP_DIM:(g + 1) * SGU_GROUP_DIM])
                 for g in range(SGU_GROUPS)], axis=1) + sb_ref[...]
            gated_ref[r0 + n * SGU_CHUNK:r0 + (n + 1) * SGU_CHUNK, :] = (
                u[rows, :] * mixed).astype(jnp.bfloat16)

    def relayout(part, z):
        r0 = part * part_rows
        for s in range(QKV_SLABS):
            zslab = z[:, s * V7X_LANES:(s + 1) * V7X_LANES]
            if s < slabs_per:
                zslab = zslab * (HEAD_DIM ** -0.5 * LOG2E)
            zs_ref[part, s] = zslab
        for s in range(QKV_SLABS):
            t, ss = divmod(s, slabs_per)
            base = t * ATTN_WIDTH + ss * V7X_LANES
            qkv1_ref[r0:r0 + part_rows, base:base + V7X_LANES] = zs_ref[part, s].astype(jnp.bfloat16)
            for c in range(RES_STEP):
                piece = zs_ref[part, s, pl.ds(c, rows4, stride=RES_STEP), :]
                zs4_ref[part, c * QKV_SLABS + s] = piece
                base = t * group_w + c * ATTN_WIDTH + ss * V7X_LANES
                qkv4_ref[part * rows4:(part + 1) * rows4, base:base + V7X_LANES] = piece.astype(jnp.bfloat16)
        for s in range(QKV_SLABS):
            t, ss = divmod(s, slabs_per)
            for c in range(RES_STEP):
                for a in range(RES_STEP):
                    piece = zs4_ref[part, c * QKV_SLABS + s, pl.ds(a, rows16, stride=RES_STEP), :]
                    base = (c * 3 + t) * group_w + a * ATTN_WIDTH + ss * V7X_LANES
                    qkv16_ref[part * rows16:(part + 1) * rows16, base:base + V7X_LANES] = (
                        piece.astype(jnp.bfloat16))

    starts = [k * part_rows for k in range(ROW_PARTS)]
    h0 = [ffn_input(r0) for r0 in starts]
    ffn = [_swiglu(h, wg_ref, wu_ref, wd_ref) for h in h0]
    h1 = [mixer_input(r0, f) for r0, f in zip(starts, ffn)]
    z = [_dot(h, win_ref[...]) for h in h1]
    for r0, zp in zip(starts, z):
        gating(r0, zp)
    for part, zp in enumerate(z):
        relayout(part, zp)


def _front(x2d, mod3, batch_offset, tiles_per_batch, n1, wg, wu, wd, nm, win, sn, sw, sb_full):
    n_tok = x2d.shape[0]
    tm = TOKEN_TILE
    row_spec = lambda rows, width: pl.BlockSpec((rows, width), lambda i: (i, 0))
    bf16 = jnp.bfloat16
    views = [(tm, n_tok, 3 * ATTN_WIDTH),
             (tm // RES_STEP, n_tok // RES_STEP, 3 * RES_STEP * ATTN_WIDTH),
             (tm // RES_STEP ** 2, n_tok // RES_STEP ** 2, 3 * RES_STEP ** 2 * ATTN_WIDTH)]
    qkv_specs = [row_spec(r, w) for r, _, w in views]
    qkv_shapes = [jax.ShapeDtypeStruct((n, w), bf16) for _, n, w in views]
    return pl.pallas_call(
        _front_kernel,
        grid=(n_tok // tm,),
        in_specs=[row_spec(tm, D_MODEL),
                  pl.BlockSpec((None, 3 * N_SUBLAYERS, D_MODEL),
                               lambda i: (batch_offset + i // tiles_per_batch, 0, 0)),
                  _const_spec((1, D_MODEL)),
                  _const_spec((D_MODEL, D_FF)), _const_spec((D_MODEL, D_FF)),
                  _const_spec((D_FF, D_MODEL)),
                  _const_spec((1, D_MODEL)),
                  _const_spec((D_MODEL, IN_WIDTH)),
                  _const_spec((1, SGU_WIDTH)),
                  _const_spec((SGU_GROUPS, SGU_CHUNK, SGU_CHUNK)),
                  _const_spec((SGU_CHUNK, SGU_WIDTH))],
        out_specs=[row_spec(tm, D_MODEL), row_spec(tm, SGU_WIDTH)] + qkv_specs,
        out_shape=[jax.ShapeDtypeStruct((n_tok, D_MODEL), jnp.float32),
                   jax.ShapeDtypeStruct((n_tok, SGU_WIDTH), bf16)] + qkv_shapes,
        scratch_shapes=[pltpu.VMEM((ROW_PARTS, QKV_SLABS, tm // ROW_PARTS, V7X_LANES), jnp.float32),
                        pltpu.VMEM((ROW_PARTS, RES_STEP * QKV_SLABS, tm // ROW_PARTS // RES_STEP, V7X_LANES),
                                   jnp.float32)],
        compiler_params=pltpu.CompilerParams(
            dimension_semantics=("arbitrary",), vmem_limit_bytes=VMEM_LIMIT),
        name="front_ffn1_inproj_sgu",
    )(x2d, mod3, n1, wg, wu, wd, nm, win, sn, sw, sb_full)


def _attn_kernel(q_ref, kp_ref, km_ref, kn_ref, vp_ref, vm_ref, vn_ref, o_ref, lse_ref,
                 bias_ref, *scratch, dil, tq, seq_len, n_res):
    interleave = n_res > 1
    n_blocks = tq // Q_BLOCK
    if interleave:
        o_s, lse_s = scratch

    i = pl.program_id(2)
    first_step = (pl.program_id(0) == 0) & (pl.program_id(1) == 0) & (i == 0)
    pair_w = 2 * Q_BLOCK

    @pl.when(first_step)
    def _():
        key = lax.broadcasted_iota(jnp.int32, (K_BLOCK, pair_w), 0)
        col = lax.broadcasted_iota(jnp.int32, (K_BLOCK, pair_w), 1)
        query = jnp.where(col < Q_BLOCK, col, col - Q_BLOCK)
        rel = jnp.abs(key - HALO - query)
        dist = (dil * rel).astype(jnp.float32)
        before_start = key < HALO
        past_end = key >= HALO + Q_BLOCK
        for pair in range(N_PAIRS):
            slope_a = 2.0 ** (-8.0 * (2 * pair + 1) / ATTN_HEADS)
            slope_b = 2.0 ** (-8.0 * (2 * pair + 2) / ATTN_HEADS)
            slope = jnp.where(col < Q_BLOCK, slope_a * LOG2E, slope_b * LOG2E)
            plain = jnp.where(rel <= HALO, -slope * dist, NEG)
            bias_ref[0, pair] = plain
            bias_ref[1, pair] = jnp.where(before_start, NEG, plain)
            bias_ref[2, pair] = jnp.where(past_end, NEG, plain)
            bias_ref[3, pair] = jnp.where(before_start | past_end, NEG, plain)

    lane = lax.broadcasted_iota(jnp.int32, (1, HEAD_PAIR), 1)
    first_head = lane < HEAD_DIM
    last_tile = seq_len // tq - 1

    def bias_variant(j):
        variant = 0
        if j == 0:
            variant = variant + jnp.where(i == 0, 1, 0)
        if j == n_blocks - 1:
            variant = variant + jnp.where(i == last_tile, 2, 0)
        return variant

    def block_operands(res, j, pair):
        qrows = slice(j * Q_BLOCK, (j + 1) * Q_BLOCK)
        cols = slice(res * ATTN_WIDTH + pair * HEAD_PAIR,
                     res * ATTN_WIDTH + (pair + 1) * HEAD_PAIR)
        return qrows, cols

    def window(prev, main, nxt, j, cols):
        lo, hi = j * Q_BLOCK - HALO, (j + 1) * Q_BLOCK + HALO
        parts = []
        if lo < 0:
            parts.append(prev[:, cols])
        parts.append(main[max(lo, 0):min(hi, tq), cols])
        if hi > tq:
            parts.append(nxt[:, cols])
        return parts[0] if len(parts) == 1 else jnp.concatenate(parts, axis=0)

    values_cache = {}

    def values_t(res, pair):
        if (res, pair) not in values_cache:
            cols = block_operands(res, 0, pair)[1]
            rows = jnp.concatenate([vp_ref[:, cols], vm_ref[:, cols], vn_ref[:, cols]], axis=0)
            values_cache[(res, pair)] = rows.T
        return values_cache[(res, pair)]

    def scores(res, j, pair):
        qrows, cols = block_operands(res, j, pair)
        qp = q_ref[qrows, cols]
        zero = jnp.zeros_like(qp)
        qstack = jnp.concatenate([jnp.where(first_head, qp, zero), jnp.where(first_head, zero, qp)], axis=0)
        kp = window(kp_ref, km_ref, kn_ref, j, cols)
        return lax.dot_general(kp, qstack, (((1,), (1,)), ((), ())),
                               preferred_element_type=jnp.float32)

    def softmax_pv(res, j, pair, s):
        qrows, cols = block_operands(res, j, pair)
        s = s + bias_ref[bias_variant(j), pair]
        m = jnp.max(s, axis=0, keepdims=True)
        p = jnp.exp2(s - m).astype(jnp.bfloat16)
        v_t = values_t(res, pair)[:, j * Q_BLOCK:j * Q_BLOCK + K_BLOCK]
        one = jnp.ones((HEAD_DIM, K_BLOCK), v_t.dtype)
        out_a = _dot(jnp.concatenate([v_t[:HEAD_DIM], one], axis=0), p[:, :Q_BLOCK])
        out_b = _dot(jnp.concatenate([one, v_t[HEAD_DIM:]], axis=0), p[:, Q_BLOCK:])
        sum_a = out_a[HEAD_DIM:HEAD_DIM + V7X_SUBLANES]
        sum_b = out_b[:V7X_SUBLANES]
        reps = HEAD_DIM // V7X_SUBLANES
        rows = lambda t: jnp.concatenate([t] * reps, axis=0)
        o_t = jnp.concatenate([out_a[:HEAD_DIM] * rows(1.0 / sum_a),
                               out_b[HEAD_DIM:] * rows(1.0 / sum_b)], axis=0)
        lse_t = jnp.concatenate([rows(m[:, :Q_BLOCK] + jnp.log2(sum_a)),
                                 rows(m[:, Q_BLOCK:] + jnp.log2(sum_b))], axis=0)
        o, lse = o_t.T, lse_t.T
        if interleave:
            dst = pl.ds(n_res * j * Q_BLOCK + res, Q_BLOCK, stride=n_res)
            o_s[pair, dst, :] = o
            lse_s[pair, dst, :] = lse
        else:
            o_ref[qrows, cols] = o.astype(o_ref.dtype)
            lse_ref[qrows, cols] = lse

    for res in range(n_res):
        for j in range(n_blocks):
            for pair in range(N_PAIRS):
                softmax_pv(res, j, pair, scores(res, j, pair))

    if interleave:
        for pair in range(N_PAIRS):
            cols = slice(pair * HEAD_PAIR, (pair + 1) * HEAD_PAIR)
            o_ref[:, cols] = o_s[pair].astype(o_ref.dtype)
            lse_ref[:, cols] = lse_s[pair]


def _attention(qkv, *, dil, n_res, tq, o_dtype):
    b, seq_len, width = qkv.shape
    n_groups = width // (3 * n_res * ATTN_WIDTH)
    blk_w = n_res * ATTN_WIDTH
    halo_blocks = tq // HALO
    last_halo = seq_len // HALO - 1
    main = lambda t: pl.BlockSpec((None, tq, blk_w), lambda bi, g, i: (bi, i, 3 * g + t))
    prv = lambda t: pl.BlockSpec(
        (None, HALO, blk_w), lambda bi, g, i: (bi, jnp.maximum(i * halo_blocks - 1, 0), 3 * g + t))
    nxt = lambda t: pl.BlockSpec(
        (None, HALO, blk_w), lambda bi, g, i: (bi, jnp.minimum((i + 1) * halo_blocks, last_halo), 3 * g + t))
    out_rows = n_res * tq
    out_spec = pl.BlockSpec((None, out_rows, ATTN_WIDTH), lambda bi, g, i: (bi, i, g))
    out_dims = (b, seq_len * n_res, n_groups * ATTN_WIDTH)

    scratch = [pltpu.VMEM((4, N_PAIRS, K_BLOCK, 2 * Q_BLOCK), jnp.float32)]
    if n_res > 1:
        scratch += [pltpu.VMEM((N_PAIRS, out_rows, HEAD_PAIR), jnp.float32)] * 2
    return pl.pallas_call(
        functools.partial(_attn_kernel, dil=dil, tq=tq, seq_len=seq_len, n_res=n_res),
        grid=(b, n_groups, seq_len // tq),
        in_specs=[main(0), prv(1), main(1), nxt(1), prv(2), main(2), nxt(2)],
        out_specs=[out_spec, out_spec],
        out_shape=[jax.ShapeDtypeStruct(out_dims, o_dtype),
                   jax.ShapeDtypeStruct(out_dims, jnp.float32)],
        scratch_shapes=scratch,
        compiler_params=pltpu.CompilerParams(
            dimension_semantics=("arbitrary", "arbitrary", "arbitrary"),
            vmem_limit_bytes=VMEM_LIMIT),
        name=f"band_attn_d{dil}",
    )(*([qkv] * 7))


def _back_kernel(x1_ref, mod_ref, gated_ref, o16_ref, l16_ref, o4_ref, l4_ref, o1_ref, l1_ref,
                 wout_ref, n2_ref, wg_ref, wu_ref, wd_ref, nf_ref, y_ref, o16n_ref, l16n_ref):
    gate1 = mod_ref[5:6, :]
    shift2, scale2, gate2 = mod_ref[6:7, :], mod_ref[7:8, :], mod_ref[8:9, :]

    slabs = ATTN_WIDTH // V7X_LANES
    rows4 = TOKEN_TILE // RES_STEP
    for c in range(RES_STEP):
        for sl in range(slabs):
            lanes = slice(c * ATTN_WIDTH + sl * V7X_LANES, c * ATTN_WIDTH + (sl + 1) * V7X_LANES)
            dst = pl.ds(c, rows4, stride=RES_STEP)
            o16n_ref[sl, dst, :] = o16_ref[:, lanes]
            l16n_ref[sl, dst, :] = l16_ref[:, lanes]

    def merged(rows):
        parts = []
        for sl in range(slabs):
            lanes = slice(sl * V7X_LANES, (sl + 1) * V7X_LANES)
            l16, l4, l1 = l16n_ref[sl, rows, :], l4_ref[rows, lanes], l1_ref[rows, lanes]
            top = jnp.maximum(jnp.maximum(l16, l4), l1)
            w16, w4, w1 = jnp.exp2(l16 - top), jnp.exp2(l4 - top), jnp.exp2(l1 - top)
            num = (w16 * o16n_ref[sl, rows, :] + w4 * o4_ref[rows, lanes].astype(jnp.float32)
                   + w1 * o1_ref[rows, lanes].astype(jnp.float32))
            parts.append((num * (1.0 / (w16 + w4 + w1))).astype(jnp.bfloat16))
        return jnp.concatenate(parts, axis=1)

    def mixed(rows, attn):
        mix = (_dot(attn, wout_ref[0:ATTN_WIDTH, :])
               + _dot(gated_ref[rows, :], wout_ref[ATTN_WIDTH:, :]))
        x2 = x1_ref[rows, :] + gate1 * mix
        h = (_rms(x2) * n2_ref[...]) * (1.0 + scale2) + shift2
        return x2, h.astype(jnp.bfloat16)

    parts = [slice(k * TOKEN_TILE // ROW_PARTS, (k + 1) * TOKEN_TILE // ROW_PARTS) for k in range(ROW_PARTS)]
    attn = [merged(rows) for rows in parts]
    staged = [mixed(rows, a) for rows, a in zip(parts, attn)]
    ffn = [_swiglu(h, wg_ref, wu_ref, wd_ref) for _, h in staged]
    for rows, (x2, _), f in zip(parts, staged, ffn):
        x3 = x2 + (FFN_RESID * gate2) * f
        y_ref[rows, :] = _rms(x3) * nf_ref[...]


def _back(x1, mod3, batch_offset, tiles_per_batch, gated, res16, res4, res1,
          wout, n2, wg, wu, wd, nf):
    n_tok = x1.shape[0]
    tm = TOKEN_TILE
    row_spec = lambda width: pl.BlockSpec((tm, width), lambda i: (i, 0))
    view4_spec = pl.BlockSpec((tm // RES_STEP, RES_STEP * ATTN_WIDTH), lambda i: (i, 0))
    slabs = ATTN_WIDTH // V7X_LANES
    return pl.pallas_call(
        _back_kernel,
        grid=(n_tok // tm,),
        in_specs=[row_spec(D_MODEL),
                  pl.BlockSpec((None, 3 * N_SUBLAYERS, D_MODEL),
                               lambda i: (batch_offset + i // tiles_per_batch, 0, 0)),
                  row_spec(SGU_WIDTH),
                  view4_spec, view4_spec,
                  row_spec(ATTN_WIDTH), row_spec(ATTN_WIDTH),
                  row_spec(ATTN_WIDTH), row_spec(ATTN_WIDTH),
                  _const_spec((D_MODEL, D_MODEL)),
                  _const_spec((1, D_MODEL)),
                  _const_spec((D_MODEL, D_FF)), _const_spec((D_MODEL, D_FF)),
                  _const_spec((D_FF, D_MODEL)),
                  _const_spec((1, D_MODEL))],
        out_specs=row_spec(D_MODEL),
        out_shape=jax.ShapeDtypeStruct((n_tok, D_MODEL), jnp.float32),
        scratch_shapes=[pltpu.VMEM((slabs, tm, V7X_LANES), jnp.float32)] * 2,
        compiler_params=pltpu.CompilerParams(
            dimension_semantics=("arbitrary",), vmem_limit_bytes=VMEM_LIMIT),
        name="back_merge_outproj_ffn2",
    )(x1, mod3, gated, *res16, *res4, *res1, wout, n2, wg, wu, wd, nf)


def kernel(x_prompt, x_sample, c_prompt, c_sample, ada_w, ada_b, ffn1_norm, ffn1_w_gate, ffn1_w_up, ffn1_w_down, mix_norm, w_in, sgu_norm, sgu_w, sgu_b, w_out, ffn2_norm, ffn2_w_gate, ffn2_w_up, ffn2_w_down, final_norm):
    bf = lambda w: w.astype(jnp.bfloat16)
    row = lambda g: g.reshape(1, -1)
    assert ada_w.shape[0] == 1, "single-layer stack"

    c_all = jnp.concatenate([c_prompt, c_sample], axis=0)
    mod3 = _modulation(c_all, ada_w[0], ada_b[0]).reshape(-1, 3 * N_SUBLAYERS, D_MODEL)

    wg1, wu1, wd1 = bf(ffn1_w_gate[0]), bf(ffn1_w_up[0]), bf(ffn1_w_down[0])
    wg2, wu2, wd2 = bf(ffn2_w_gate[0]), bf(ffn2_w_up[0]), bf(ffn2_w_down[0])
    win, wout, sw = bf(w_in[0]), bf(w_out[0]), bf(sgu_w[0])
    sb_full = jnp.repeat(sgu_b[0].T, SGU_GROUP_DIM, axis=1)

    def encode(x, batch_offset):
        b, s, _ = x.shape
        tiles_per_batch = s // TOKEN_TILE
        x2d = x.reshape(b * s, D_MODEL)
        x1, gated, qkv1, qkv4, qkv16 = _front(
            x2d, mod3, batch_offset, tiles_per_batch, row(ffn1_norm[0]), wg1, wu1, wd1,
            row(mix_norm[0]), win, row(sgu_norm[0]), sw, sb_full)
        per_batch = lambda t: t.reshape(b, t.shape[0] // b, t.shape[1])
        res16 = _attention(per_batch(qkv16), dil=16, n_res=RES_STEP, tq=2 * Q_BLOCK, o_dtype=jnp.float32)
        res4 = _attention(per_batch(qkv4), dil=4, n_res=RES_STEP, tq=2 * Q_BLOCK, o_dtype=jnp.bfloat16)
        res1 = _attention(per_batch(qkv1), dil=1, n_res=1, tq=8 * Q_BLOCK, o_dtype=jnp.bfloat16)
        flat = lambda t: t.reshape(t.shape[0] * t.shape[1], t.shape[2])
        y = _back(x1, mod3, batch_offset, tiles_per_batch, gated,
                  [flat(t) for t in res16], [flat(t) for t in res4], [flat(t) for t in res1],
                  wout, row(ffn2_norm[0]), wg2, wu2, wd2, row(final_norm))
        return y.reshape(b, s, D_MODEL)

    y_prompt = encode(x_prompt, 0)
    y_sample = encode(x_sample, x_prompt.shape[0])
    return (y_prompt, y_sample)
```
